```python
import jax, jax.numpy as jnp
from jax import lax
import numpy as np

D_MODEL = 1024
BATCH = 2
SEQ = 16384
DEPTH = 2
DEC_BATCH = 8
DEC_SEQ = 32
PAST_LEN = 1024

CHUNK = 64
Q_BLOCK = 128
N_EVEN = (DEPTH + 1) // 2
N_ODD = DEPTH // 2
FOX_HEADS = 8
FOX_DH = 64
FOX_W = FOX_HEADS * FOX_DH
FOX_FORGET_BIAS = 4.0
GDN_HEADS = 4
GDN_DK = 128
GDN_DV = 128
GDN_QK = GDN_HEADS * GDN_DK
GDN_VW = GDN_HEADS * GDN_DV
GDN_CONV_DIM = 2 * GDN_QK + GDN_VW
CONV_W = 4
S5_WIDTH = D_MODEL
S5_GROUP = 16
S5_GROUPS = S5_WIDTH // S5_GROUP
S5_STATE = 64
D_FF = ((8 * D_MODEL // 3 + 127) // 128) * 128
ALPHA = (2.0 * DEPTH) ** 0.25
BETA_INIT = (8.0 * DEPTH) ** -0.25
LN_EPS = 1e-5
NORM_EPS = 1e-6
EVEN_SPLITS = (FOX_W, FOX_W, FOX_W, FOX_HEADS, GDN_CONV_DIM, GDN_HEADS, GDN_HEADS, GDN_VW)
EVEN_IN = sum(EVEN_SPLITS)
MIX_W = FOX_W + GDN_VW
F32 = jnp.float32

kernel_name = 'hybrid_fox_gdn_s5_stream_step'


def layer_norm(x, g, b):
    xf = x.astype(F32)
    mu = jnp.mean(xf, -1, keepdims=True)
    var = jnp.mean(jnp.square(xf - mu), -1, keepdims=True)
    return ((xf - mu) * lax.rsqrt(var + LN_EPS) * g.astype(F32) + b.astype(F32)).astype(x.dtype)


def swiglu(x, w_in, w_out):
    a, b = jnp.split(x @ w_in, 2, axis=-1)
    return (jax.nn.silu(a) * b) @ w_out


def macaron_half(x, w_in, w_out, g, b):
    return layer_norm(ALPHA * x + 0.5 * swiglu(x, w_in, w_out), g, b)


def split_cols(y, sizes):
    return jnp.split(y, np.cumsum(sizes)[:-1].tolist(), axis=-1)


def l2norm(x):
    return x * lax.rsqrt(jnp.sum(jnp.square(x), -1, keepdims=True) + NORM_EPS)


def fox_prompt(q, k, v, logf):
    B, T, H, dh = q.shape
    nblk = T // Q_BLOCK
    c = jnp.cumsum(logf, axis=1).transpose(0, 2, 1)
    qb = q.reshape(B, nblk, Q_BLOCK, H, dh).transpose(1, 0, 2, 3, 4)
    cb = c.reshape(B, H, nblk, Q_BLOCK).transpose(2, 0, 1, 3)
    kpos = jnp.arange(T)

    def block(args):
        i, q_i, c_i = args
        s = jnp.einsum('bqhd,bkhd->bhqk', q_i, k, preferred_element_type=F32) * (dh ** -0.5)
        s = s + c_i[..., :, None] - c[..., None, :]
        mask = kpos[None, :] <= (i * Q_BLOCK + jnp.arange(Q_BLOCK))[:, None]
        p = jax.nn.softmax(jnp.where(mask, s, -jnp.inf), axis=-1)
        return jnp.einsum('bhqk,bkhd->bqhd', p.astype(v.dtype), v)

    o = lax.map(block, (jnp.arange(nblk), qb, cb))
    return o.transpose(1, 0, 2, 3, 4).reshape(B, T, H, dh)


def fox_sample(q, k, v, logf, past_k, past_v, past_logf):
    T = q.shape[1]
    P = past_k.shape[1]
    kk = jnp.concatenate([past_k.astype(k.dtype), k], 1)
    vv = jnp.concatenate([past_v.astype(v.dtype), v], 1)
    c = jnp.cumsum(jnp.concatenate([past_logf.astype(F32), logf], 1), axis=1).transpose(0, 2, 1)
    s = jnp.einsum('bqhd,bkhd->bhqk', q, kk, preferred_element_type=F32) * (FOX_DH ** -0.5)
    s = s + c[..., P:, None] - c[..., None, :]
    mask = jnp.arange(P + T)[None, :] <= (P + jnp.arange(T))[:, None]
    p = jax.nn.softmax(jnp.where(mask, s, -jnp.inf), axis=-1)
    return jnp.einsum('bhqk,bkhd->bqhd', p.astype(vv.dtype), vv)


def causal_conv(x, buf, w):
    xp = jnp.concatenate([buf.astype(x.dtype), x], 1)
    y = lax.conv_general_dilated(xp, w[:, None, :].astype(x.dtype), window_strides=(1,), padding='VALID',
                                 dimension_numbers=('NWC', 'WIO', 'NWC'), feature_group_count=x.shape[-1])
    return y, xp[:, -(CONV_W - 1):]


def gdn_chunked(q, k, v, g, beta, s0, L):
    B, T, H, _ = q.shape
    dv = v.shape[-1]
    n = T // L

    def chunks(x):
        return jnp.moveaxis(x.reshape((B, n, L, H) + x.shape[3:]), 3, 2)

    qc, kc, vc, gc, bc = chunks(q), chunks(k), chunks(v), chunks(g), chunks(beta)
    gcum = jnp.cumsum(gc, -1)
    diff = gcum[..., :, None] - gcum[..., None, :]
    tril = jnp.tril(jnp.ones((L, L), bool))
    strict = jnp.tril(jnp.ones((L, L), bool), -1)
    decay = jnp.where(tril, jnp.exp(jnp.where(tril, diff, 0.0)), 0.0)
    kb = kc * bc[..., None]
    nmat = jnp.where(strict, jnp.einsum('bnhid,bnhjd->bnhij', kb, kc) * decay, 0.0)
    eye = jnp.eye(L, dtype=F32)
    tmat = lax.linalg.triangular_solve(nmat, jnp.broadcast_to(eye, nmat.shape), left_side=True,
                                       lower=True, unit_diagonal=True)
    u = tmat @ (vc * bc[..., None])
    w = tmat @ (kb * jnp.exp(gcum)[..., None])
    attn = jnp.einsum('bnhid,bnhjd->bnhij', qc, kc) * decay
    qg = qc * jnp.exp(gcum)[..., None]
    kdec = kc * jnp.exp(gcum[..., -1:] - gcum)[..., None]
    glast = jnp.exp(gcum[..., -1])

    def step(s, inp):
        u_i, w_i, a_i, qg_i, kd_i, gl_i = inp
        v_new = u_i - w_i @ s
        o_i = qg_i @ s + a_i @ v_new
        s = s * gl_i[..., None, None] + jnp.swapaxes(kd_i, -1, -2) @ v_new
        return s, o_i

    xs = tuple(jnp.moveaxis(t, 1, 0) for t in (u, w, attn, qg, kdec, glast))
    s_fin, o = lax.scan(step, s0, xs)
    return o.transpose(1, 0, 3, 2, 4).reshape(B, T, H, dv), s_fin


def even_mixer(h, fox_past, gdn_s0, gdn_buf, w_in, b_f, conv_w, a_log, dt_bias, norm_g, w_out):
    Bt, T, _ = h.shape
    fq, fk, fv, ff, gqkv, ga, gb, gz = split_cols(h @ w_in, EVEN_SPLITS)
    fq = fq.reshape(Bt, T, FOX_HEADS, FOX_DH)
    fk = fk.reshape(Bt, T, FOX_HEADS, FOX_DH)
    fv = fv.reshape(Bt, T, FOX_HEADS, FOX_DH)
    logf = jax.nn.log_sigmoid((ff + b_f).astype(F32))
    if fox_past is None:
        o_fox = fox_prompt(fq, fk, fv, logf)
    else:
        o_fox = fox_sample(fq, fk, fv, logf, *fox_past)
    conv_out, new_buf = causal_conv(gqkv, gdn_buf, conv_w)
    gq, gk, gv = split_cols(jax.nn.silu(conv_out.astype(F32)), (GDN_QK, GDN_QK, GDN_VW))
    gq = l2norm(gq.reshape(Bt, T, GDN_HEADS, GDN_DK)) * (GDN_DK ** -0.5)
    gk = l2norm(gk.reshape(Bt, T, GDN_HEADS, GDN_DK))
    gv = gv.reshape(Bt, T, GDN_HEADS, GDN_DV)
    g = -jnp.exp(a_log.astype(F32)) * jax.nn.softplus(ga.astype(F32) + dt_bias.astype(F32))
    beta = jax.nn.sigmoid(gb.astype(F32))
    o_gdn, s_new = gdn_chunked(gq, gk, gv, g, beta, gdn_s0.astype(F32), min(T, CHUNK))
    z = gz.reshape(Bt, T, GDN_HEADS, GDN_DV).astype(F32)
    o_gdn = o_gdn * lax.rsqrt(jnp.mean(jnp.square(o_gdn), -1, keepdims=True) + NORM_EPS) * norm_g.astype(F32) * jax.nn.silu(z)
    mixed = jnp.concatenate([o_fox.reshape(Bt, T, FOX_W).astype(h.dtype),
                             o_gdn.reshape(Bt, T, GDN_VW).astype(h.dtype)], -1)
    return mixed @ w_out, (fk, fv, logf, s_new, new_buf)


def s5_scan(u, h0_re, h0_im, lam_re, lam_im, log_step, b_re, b_im, c_re, c_im, d):
    Bt, T, _ = u.shape
    L = min(T, CHUNK)
    n = T // L
    lam = lax.complex(lam_re.astype(F32), lam_im.astype(F32))
    lam_bar = jnp.exp(lam * jnp.exp(log_step.astype(F32))[:, None])
    b_bar = ((lam_bar - 1.0) / lam)[..., None] * lax.complex(b_re.astype(F32), b_im.astype(F32))
    c_mat = lax.complex(c_re.astype(F32), c_im.astype(F32))
    uc = u.reshape(Bt, n, L, S5_GROUPS, S5_GROUP).transpose(1, 0, 2, 3, 4).astype(jnp.complex64)

    def combine(e1, e2):
        a1, b1 = e1
        a2, b2 = e2
        return a1 * a2, a2 * b1 + b2

    def step(hc, u_i):
        bu = jnp.einsum('gpc,blgc->blgp', b_bar, u_i)
        bu = bu.at[:, 0].add(lam_bar * hc)
        _, states = lax.associative_scan(combine, (jnp.broadcast_to(lam_bar, bu.shape), bu), axis=1)
        y = jnp.real(jnp.einsum('gcp,blgp->blgc', c_mat, states))
        return states[:, -1], y

    h_fin, ys = lax.scan(step, lax.complex(h0_re.astype(F32), h0_im.astype(F32)), uc)
    y = ys.transpose(1, 0, 2, 3, 4).reshape(Bt, T, S5_WIDTH) + d.astype(F32) * u
    return y, jnp.real(h_fin), jnp.imag(h_fin)


def odd_mixer(h, h0_re, h0_im, w_in, lam_re, lam_im, log_step, b_re, b_im, c_re, c_im, d, glu_w, glu_b, w_out):
    u = (h @ w_in).astype(F32)
    y, h_re, h_im = s5_scan(u, h0_re, h0_im, lam_re, lam_im, log_step, b_re, b_im, c_re, c_im, d)
    zz = jax.nn.gelu(y)
    gated = zz * jax.nn.sigmoid(zz @ glu_w.astype(F32) + glu_b.astype(F32))
    return gated.astype(h.dtype) @ w_out, (h_re, h_im)


def stack_field(states, idx):
    return jnp.stack([s[idx] for s in states])


def setup_inputs(seed: int = 0) -> dict:
    key = jax.random.key(seed)
    ks = iter(jax.random.split(key, 40))

    def nrm(shape, scale):
        return scale * jax.random.normal(next(ks), shape, F32)

    def unif(shape, lo, hi):
        return jax.random.uniform(next(ks), shape, F32, lo, hi)

    x_prompt = nrm((BATCH, SEQ, D_MODEL), 1.0)
    x_sample = nrm((DEC_BATCH, DEC_SEQ, D_MODEL), 1.0)
    cache_fox_k = nrm((N_EVEN, DEC_BATCH, PAST_LEN, FOX_HEADS, FOX_DH), 1.0)
    cache_fox_v = nrm((N_EVEN, DEC_BATCH, PAST_LEN, FOX_HEADS, FOX_DH), 1.0)
    cache_fox_logf = jax.nn.log_sigmoid(FOX_FORGET_BIAS + nrm((N_EVEN, DEC_BATCH, PAST_LEN, FOX_HEADS), 1.0))
    state_gdn = nrm((N_EVEN, DEC_BATCH, GDN_HEADS, GDN_DK, GDN_DV), 0.1)
    state_gdn_conv = nrm((N_EVEN, DEC_BATCH, CONV_W - 1, GDN_CONV_DIM), 1.0)
    state_s5_re = nrm((N_ODD, DEC_BATCH, S5_GROUPS, S5_STATE), 0.1)
    state_s5_im = nrm((N_ODD, DEC_BATCH, S5_GROUPS, S5_STATE), 0.1)
    ffn_w_in = nrm((DEPTH, 2, D_MODEL, 2 * D_FF), D_MODEL ** -0.5)
    ffn_w_out = nrm((DEPTH, 2, D_FF, D_MODEL), BETA_INIT * D_FF ** -0.5)
    ln_g = 1.0 + nrm((DEPTH, 3, D_MODEL), 0.02)
    ln_b = nrm((DEPTH, 3, D_MODEL), 0.02)
    even_w_in = nrm((N_EVEN, D_MODEL, EVEN_IN), D_MODEL ** -0.5)
    fox_b_f = FOX_FORGET_BIAS + nrm((N_EVEN, FOX_HEADS), 0.1)
    gdn_conv_w = nrm((N_EVEN, CONV_W, GDN_CONV_DIM), CONV_W ** -0.5)
    gdn_a_log = jnp.log(unif((N_EVEN, GDN_HEADS), 1.0, 16.0))
    dt = jnp.exp(unif((N_EVEN, GDN_HEADS), float(np.log(1e-3)), float(np.log(1e-1))))
    gdn_dt_bias = dt + jnp.log(-jnp.expm1(-dt))
    gdn_norm_g = 1.0 + nrm((N_EVEN, GDN_DV), 0.02)
    even_w_out = nrm((N_EVEN, MIX_W, D_MODEL), BETA_INIT * MIX_W ** -0.5)
    odd_w_in = nrm((N_ODD, D_MODEL, S5_WIDTH), D_MODEL ** -0.5)
    s5_lam_re = -0.5 + nrm((N_ODD, S5_GROUPS, S5_STATE), 0.01)
    s5_lam_im = float(np.pi) * jnp.arange(S5_STATE, dtype=F32) + nrm((N_ODD, S5_GROUPS, S5_STATE), 0.01)
    s5_log_step = unif((N_ODD, S5_GROUPS), float(np.log(1e-3)), float(np.log(1e-1)))
    s5_b_re = nrm((N_ODD, S5_GROUPS, S5_STATE, S5_GROUP), (2.0 * S5_GROUP) ** -0.5)
    s5_b_im = nrm((N_ODD, S5_GROUPS, S5_STATE, S5_GROUP), (2.0 * S5_GROUP) ** -0.5)
    s5_c_re = nrm((N_ODD, S5_GROUPS, S5_GROUP, S5_STATE), (2.0 * S5_STATE) ** -0.5)
    s5_c_im = nrm((N_ODD, S5_GROUPS, S5_GROUP, S5_STATE), (2.0 * S5_STATE) ** -0.5)
    s5_d = nrm((N_ODD, S5_WIDTH), 1.0)
    s5_glu_w = nrm((N_ODD, S5_WIDTH, S5_WIDTH), S5_WIDTH ** -0.5)
    s5_glu_b = nrm((N_ODD, S5_WIDTH), 0.02)
    odd_w_out = nrm((N_ODD, S5_WIDTH, D_MODEL), BETA_INIT * S5_WIDTH ** -0.5)
    return {'x_prompt': x_prompt, 'x_sample': x_sample, 'cache_fox_k': cache_fox_k, 'cache_fox_v': cache_fox_v,
            'cache_fox_logf': cache_fox_logf, 'state_gdn': state_gdn, 'state_gdn_conv': state_gdn_conv,
            'state_s5_re': state_s5_re, 'state_s5_im': state_s5_im, 'ffn_w_in': ffn_w_in, 'ffn_w_out': ffn_w_out,
            'ln_g': ln_g, 'ln_b': ln_b, 'even_w_in': even_w_in, 'fox_b_f': fox_b_f, 'gdn_conv_w': gdn_conv_w,
            'gdn_a_log': gdn_a_log, 'gdn_dt_bias': gdn_dt_bias, 'gdn_norm_g': gdn_norm_g, 'even_w_out': even_w_out,
            'odd_w_in': odd_w_in, 's5_lam_re': s5_lam_re, 's5_lam_im': s5_lam_im, 's5_log_step': s5_log_step,
            's5_b_re': s5_b_re, 's5_b_im': s5_b_im, 's5_c_re': s5_c_re, 's5_c_im': s5_c_im, 's5_d': s5_d,
            's5_glu_w': s5_glu_w, 's5_glu_b': s5_glu_b, 'odd_w_out': odd_w_out}


def reference(x_prompt, x_sample, cache_fox_k, cache_fox_v, cache_fox_logf, state_gdn, state_gdn_conv,
              state_s5_re, state_s5_im, ffn_w_in, ffn_w_out, ln_g, ln_b, even_w_in, fox_b_f, gdn_conv_w,
              gdn_a_log, gdn_dt_bias, gdn_norm_g, even_w_out, odd_w_in, s5_lam_re, s5_lam_im, s5_log_step,
              s5_b_re, s5_b_im, s5_c_re, s5_c_im, s5_d, s5_glu_w, s5_glu_b, odd_w_out):
    xp, xs = x_prompt, x_sample
    bp = xp.shape[0]
    even_p, even_s, odd_p, odd_s = [], [], [], []
    for layer in range(DEPTH):
        xp = macaron_half(xp, ffn_w_in[layer, 0], ffn_w_out[layer, 0], ln_g[layer, 0], ln_b[layer, 0])
        xs = macaron_half(xs, ffn_w_in[layer, 0], ffn_w_out[layer, 0], ln_g[layer, 0], ln_b[layer, 0])
        j = layer // 2
        if layer % 2 == 0:
            ew = (even_w_in[j], fox_b_f[j], gdn_conv_w[j], gdn_a_log[j], gdn_dt_bias[j], gdn_norm_g[j], even_w_out[j])
            s0 = jnp.zeros((bp, GDN_HEADS, GDN_DK, GDN_DV), F32)
            buf0 = jnp.zeros((bp, CONV_W - 1, GDN_CONV_DIM), xp.dtype)
            mp, st_p = even_mixer(xp, None, s0, buf0, *ew)
            ms, st_s = even_mixer(xs, (cache_fox_k[j], cache_fox_v[j], cache_fox_logf[j]),
                                  state_gdn[j], state_gdn_conv[j], *ew)
            even_p.append(st_p)
            even_s.append(st_s)
        else:
            ow = (odd_w_in[j], s5_lam_re[j], s5_lam_im[j], s5_log_step[j], s5_b_re[j], s5_b_im[j],
                  s5_c_re[j], s5_c_im[j], s5_d[j], s5_glu_w[j], s5_glu_b[j], odd_w_out[j])
            h0 = jnp.zeros((bp, S5_GROUPS, S5_STATE), F32)
            mp, st_p = odd_mixer(xp, h0, h0, *ow)
            ms, st_s = odd_mixer(xs, state_s5_re[j], state_s5_im[j], *ow)
            odd_p.append(st_p)
            odd_s.append(st_s)
        xp = layer_norm(ALPHA * xp + mp, ln_g[layer, 1], ln_b[layer, 1])
        xs = layer_norm(ALPHA * xs + ms, ln_g[layer, 1], ln_b[layer, 1])
        xp = macaron_half(xp, ffn_w_in[layer, 1], ffn_w_out[layer, 1], ln_g[layer, 2], ln_b[layer, 2])
        xs = macaron_half(xs, ffn_w_in[layer, 1], ffn_w_out[layer, 1], ln_g[layer, 2], ln_b[layer, 2])
    y_prompt, y_sample = xp, xs
    fox_k_prompt = stack_field(even_p, 0)
    fox_v_prompt = stack_field(even_p, 1)
    fox_logf_prompt = stack_field(even_p, 2)
    gdn_state_prompt = stack_field(even_p, 3)
    gdn_conv_prompt = stack_field(even_p, 4)
    s5_re_prompt = stack_field(odd_p, 0)
    s5_im_prompt = stack_field(odd_p, 1)
    fox_k_sample = stack_field(even_s, 0)
    fox_v_sample = stack_field(even_s, 1)
    fox_logf_sample = stack_field(even_s, 2)
    gdn_state_sample = stack_field(even_s, 3)
    gdn_conv_sample = stack_field(even_s, 4)
    s5_re_sample = stack_field(odd_s, 0)
    s5_im_sample = stack_field(odd_s, 1)
    return (y_prompt, y_sample, fox_k_prompt, fox_v_prompt, fox_logf_prompt, gdn_state_prompt, gdn_conv_prompt,
            s5_re_prompt, s5_im_prompt, fox_k_sample, fox_v_sample, fox_logf_sample, gdn_state_sample,
            gdn_conv_sample, s5_re_sample, s5_im_sample)
```

```python
import functools

import jax
import jax.numpy as jnp
from jax import lax
from jax.experimental import pallas as pl
from jax.experimental.pallas import tpu as pltpu

F32 = jnp.float32
BF16 = jnp.bfloat16

D_MODEL = 1024
DEPTH = 2
CHUNK = 64
FOX_HEADS = 8
FOX_DH = 64
FOX_W = FOX_HEADS * FOX_DH
GDN_HEADS = 4
GDN_DK = 128
GDN_DV = 128
GDN_QK = GDN_HEADS * GDN_DK
GDN_VW = GDN_HEADS * GDN_DV
GDN_CONV_DIM = 2 * GDN_QK + GDN_VW
CONV_W = 4
S5_GROUP = 16
S5_GROUPS = D_MODEL // S5_GROUP
S5_STATE = 64
S5_LANES = S5_GROUPS * S5_STATE
D_FF = ((8 * D_MODEL // 3 + 127) // 128) * 128
ALPHA = (2.0 * DEPTH) ** 0.25
LN_EPS = 1e-5
NORM_EPS = 1e-6
EVEN_SPLITS = (FOX_W, FOX_W, FOX_W, FOX_HEADS, GDN_CONV_DIM, GDN_HEADS, GDN_HEADS, GDN_VW)

LANES = 128
SUBLANES = 8
MXU_DIM = 256
VMEM_LIMIT = 56 * 1024 * 1024

_Q0, _K0, _V0, _X0, _Z0, _G0 = 0, FOX_W, 2 * FOX_W, 3 * FOX_W, 3 * FOX_W + GDN_CONV_DIM, 3 * FOX_W + GDN_CONV_DIM + GDN_VW
EVEN_COLS = _G0 + LANES
_LOGF_LANE, _G_LANE, _BETA_LANE, _CUM_LANE = 0, FOX_HEADS, FOX_HEADS + GDN_HEADS, FOX_HEADS + 2 * GDN_HEADS
NEG_BIG = -1e30

FFN_ROWS = 512
FFN_COLS = 256
EVEN_ROWS = 512
ATT_ROWS = 512
GDN_ROWS = 256
ODD_ROWS = 256
S5_COLS = 1024


def _dot(a, b):
    return jnp.dot(a, b, preferred_element_type=F32)


def _dot_nt(a, b):
    return lax.dot_general(a, b, (((1,), (1,)), ((), ())), preferred_element_type=F32)


def _layer_norm(z, g, b):
    mu = jnp.mean(z, -1, keepdims=True)
    d = z - mu
    var = jnp.mean(d * d, -1, keepdims=True)
    return d * lax.rsqrt(var + LN_EPS) * g + b


def _split3(x):
    hi = x.astype(BF16)
    r = x - hi.astype(F32)
    mid = r.astype(BF16)
    lo = (r - mid.astype(F32)).astype(BF16)
    return hi, mid, lo


def _tri_cumsum(tri, x):
    hi, mid, lo = _split3(x)
    return _dot(tri, hi) + _dot(tri, mid) + _dot(tri, lo)


def _const_spec(shape):
    nd = len(shape)
    return pl.BlockSpec(shape, lambda *_: (0,) * nd, pipeline_mode=pl.Buffered(1))


def _params(sem):
    return pltpu.CompilerParams(dimension_semantics=sem, vmem_limit_bytes=VMEM_LIMIT)


def _ffn_kernel(x_ref, win_ref, wout_ref, g_ref, b_ref, o_ref, h_ref):
    x = x_ref[...]
    xb = x.astype(BF16)
    for c in range(D_FF // FFN_COLS):
        lo = c * FFN_COLS
        a = _dot(xb, win_ref[:, lo:lo + FFN_COLS])
        b = _dot(xb, win_ref[:, D_FF + lo:D_FF + lo + FFN_COLS])
        h_ref[:, lo:lo + FFN_COLS] = (jax.nn.silu(a) * b).astype(BF16)
    y = _dot(h_ref[...], wout_ref[...])
    o_ref[...] = _layer_norm(ALPHA * x + 0.5 * y, g_ref[...], b_ref[...])


def _ffn(x, w_in, w_out, g, b):
    n = x.shape[0]
    tm = min(FFN_ROWS, n)
    return pl.pallas_call(
        _ffn_kernel,
        grid=(n // tm,),
        in_specs=[pl.BlockSpec((tm, D_MODEL), lambda i: (i, 0)),
                  _const_spec((D_MODEL, 2 * D_FF)),
                  _const_spec((D_FF, D_MODEL)),
                  _const_spec((1, D_MODEL)),
                  _const_spec((1, D_MODEL))],
        out_specs=pl.BlockSpec((tm, D_MODEL), lambda i: (i, 0)),
        out_shape=jax.ShapeDtypeStruct((n, D_MODEL), F32),
        scratch_shapes=[pltpu.VMEM((tm, D_FF), BF16)],
        compiler_params=_params(("parallel",)),
        name="ffn_ln",
    )(x, w_in, w_out, g, b)


def _even_in_kernel(x_ref, w_ref, bias_ref, alog_ref,
                    q_ref, k_ref, v_ref, kb_ref, vb_ref, gx_ref, gz_ref, gates_ref, ct_ref,
                    carry_ref):
    t = pl.program_id(1)
    tm = x_ref.shape[1]

    @pl.when(t == 0)
    def _():
        carry_ref[...] = jnp.zeros_like(carry_ref)

    xb = x_ref[0].astype(BF16)

    def mm(lo, width):
        return _dot(xb, w_ref[:, lo:lo + width])

    q = mm(_Q0, FOX_W) * (FOX_DH ** -0.5)
    lane = lax.broadcasted_iota(jnp.int32, (tm, LANES), 1)
    for p in range(FOX_HEADS // 2):
        pair = q[:, LANES * p:LANES * (p + 1)]
        q_ref[0, :, 2 * LANES * p:2 * LANES * p + LANES] = jnp.where(lane < FOX_DH, pair, 0.0).astype(BF16)
        q_ref[0, :, 2 * LANES * p + LANES:2 * LANES * (p + 1)] = jnp.where(lane >= FOX_DH, pair, 0.0).astype(BF16)
    k = mm(_K0, FOX_W)
    k_ref[0] = k
    kb_ref[0] = k.astype(BF16)
    v = mm(_V0, FOX_W)
    v_ref[0] = v
    vb_ref[0] = v.astype(BF16)
    gx_ref[0] = mm(_X0, GDN_CONV_DIM)
    gz_ref[0] = mm(_Z0, GDN_VW)

    raw = mm(_G0, LANES) + bias_ref[...]
    logf = jax.nn.log_sigmoid(raw)
    gdec = -jnp.exp(alog_ref[...]) * jax.nn.softplus(raw)
    beta = jax.nn.sigmoid(raw)
    gates = jnp.where(lane < _G_LANE, logf, jnp.where(lane < _BETA_LANE, gdec, jnp.where(lane < _CUM_LANE, beta, logf)))
    row = lax.broadcasted_iota(jnp.int32, (tm, tm), 0)
    col = lax.broadcasted_iota(jnp.int32, (tm, tm), 1)
    tri = jnp.where(row >= col, 1.0, 0.0).astype(BF16)
    cum = _tri_cumsum(tri, logf) + carry_ref[...]
    carry_ref[...] = cum[tm - 1:tm, :]
    gates = jnp.where((lane >= _CUM_LANE) & (lane < _CUM_LANE + FOX_HEADS), cum, gates)
    gates_ref[0] = gates
    ct_ref[0] = gates.T[_CUM_LANE:_CUM_LANE + FOX_HEADS, :]


def _even_in(x, w, bias, alog):
    bsz, t, _ = x.shape
    tm = min(EVEN_ROWS, t)
    row_spec = lambda width: pl.BlockSpec((1, tm, width), lambda b, i: (b, i, 0))
    shp = lambda width, dt: jax.ShapeDtypeStruct((bsz, t, width), dt)
    return pl.pallas_call(
        _even_in_kernel,
        grid=(bsz, t // tm),
        in_specs=[row_spec(D_MODEL), _const_spec((D_MODEL, EVEN_COLS)), _const_spec((1, LANES)), _const_spec((1, LANES))],
        out_specs=[row_spec(2 * FOX_W), row_spec(FOX_W), row_spec(FOX_W), row_spec(FOX_W), row_spec(FOX_W),
                   row_spec(GDN_CONV_DIM), row_spec(GDN_VW), row_spec(LANES),
                   pl.BlockSpec((1, FOX_HEADS, tm), lambda b, i: (b, 0, i))],
        out_shape=[shp(2 * FOX_W, BF16), shp(FOX_W, F32), shp(FOX_W, F32), shp(FOX_W, BF16), shp(FOX_W, BF16),
                   shp(GDN_CONV_DIM, F32), shp(GDN_VW, F32), shp(LANES, F32),
                   jax.ShapeDtypeStruct((bsz, FOX_HEADS, t), F32)],
        scratch_shapes=[pltpu.VMEM((1, LANES), F32)],
        compiler_params=_params(("parallel", "arbitrary")),
        name="even_in",
    )(x, w, bias, alog)


def _fox_prompt_kernel(q_ref, k_ref, v_ref, cq_ref, ct_ref, o_ref, m_ref, l_ref, acc_ref):
    pair = pl.program_id(1)
    qi = pl.program_id(2)
    tq = q_ref.shape[1]
    lane = lax.broadcasted_iota(jnp.int32, (tq, LANES), 1)
    cq_blk = cq_ref[0]
    cq = [jnp.sum(jnp.where(lane == _CUM_LANE + 2 * pair + a, cq_blk, 0.0), axis=1, keepdims=True) for a in (0, 1)]
    qs = [q_ref[0, :, 0:LANES], q_ref[0, :, LANES:2 * LANES]]
    m_ref[...] = jnp.full(m_ref.shape, NEG_BIG, F32)
    l_ref[...] = jnp.zeros_like(l_ref)
    acc_ref[...] = jnp.zeros_like(acc_ref)
    row = lax.broadcasted_iota(jnp.int32, (tq, tq), 0)
    col = lax.broadcasted_iota(jnp.int32, (tq, tq), 1)

    def tile(j, masked):
        off = pl.multiple_of(j * tq, tq)
        kt = k_ref[0, pl.ds(off, tq), :]
        vt = v_ref[0, pl.ds(off, tq), :]
        for a in (0, 1):
            ck = ct_ref[0, 0, a:a + 1, pl.ds(off, tq)]
            s = (_dot_nt(qs[a], kt) + cq[a]) - ck
            if masked:
                s = jnp.where(col <= row, s, NEG_BIG)
            m_prev = m_ref[a]
            m_new = jnp.maximum(m_prev, jnp.max(s, axis=1, keepdims=True))
            scale = jnp.exp(m_prev - m_new)
            p = jnp.exp(s - m_new)
            l_ref[a] = scale * l_ref[a] + jnp.sum(p, axis=1, keepdims=True)
            acc_ref[a] = scale * acc_ref[a] + _dot(p.astype(BF16), vt)
            m_ref[a] = m_new

    def body(j, c):
        tile(j, False)
        return c

    lax.fori_loop(0, qi, body, 0)
    tile(qi, True)
    o_ref[0] = jnp.where(lane < FOX_DH, acc_ref[0] / l_ref[0], acc_ref[1] / l_ref[1]).astype(BF16)


def _fox_prompt(q, kb, vb, gates, ct):
    bsz, t, _ = kb.shape
    tq = min(ATT_ROWS, t)
    npair = FOX_HEADS // 2
    ct4 = ct.reshape(bsz, npair, 2, t)
    return pl.pallas_call(
        _fox_prompt_kernel,
        grid=(bsz, npair, t // tq),
        in_specs=[pl.BlockSpec((1, tq, 2 * LANES), lambda b, p, i: (b, i, p)),
                  pl.BlockSpec((1, t, LANES), lambda b, p, i: (b, 0, p)),
                  pl.BlockSpec((1, t, LANES), lambda b, p, i: (b, 0, p)),
                  pl.BlockSpec((1, tq, LANES), lambda b, p, i: (b, i, 0)),
                  pl.BlockSpec((1, 1, 2, t), lambda b, p, i: (b, p, 0, 0))],
        out_specs=pl.BlockSpec((1, tq, LANES), lambda b, p, i: (b, i, p)),
        out_shape=jax.ShapeDtypeStruct((bsz, t, FOX_W), BF16),
        scratch_shapes=[pltpu.VMEM((2, tq, 1), F32), pltpu.VMEM((2, tq, 1), F32), pltpu.VMEM((2, tq, LANES), F32)],
        compiler_params=_params(("parallel", "parallel", "arbitrary")),
        name="fox_prompt",
    )(q, kb, vb, gates, ct4)


def _fox_sample_kernel(q_ref, kn_ref, vn_ref, gates_ref, kc_ref, vc_ref, lc_ref, o_ref):
    past = kc_ref.shape[1]
    t = q_ref.shape[1]
    row = lax.broadcasted_iota(jnp.int32, (past, past), 0)
    col = lax.broadcasted_iota(jnp.int32, (past, past), 1)
    tri = jnp.where(row >= col, 1.0, 0.0).astype(BF16)
    cum = _tri_cumsum(tri, lc_ref[0])
    ck_cache = (cum - cum[past - 1:past, :]).T
    gts = gates_ref[0]
    ck_new = gts.T
    kc = kc_ref[0].astype(BF16)
    vc = vc_ref[0].astype(BF16)
    kn = kn_ref[0]
    vn = vn_ref[0]
    causal = lax.broadcasted_iota(jnp.int32, (t, t), 1) <= lax.broadcasted_iota(jnp.int32, (t, t), 0)
    lane = lax.broadcasted_iota(jnp.int32, (t, LANES), 1)
    outs = []
    for h in range(FOX_HEADS):
        ls = slice(LANES * (h // 2), LANES * (h // 2 + 1))
        qh = q_ref[0, :, LANES * h:LANES * (h + 1)]
        cq = gts[:, _CUM_LANE + h:_CUM_LANE + h + 1]
        s1 = (_dot_nt(qh, kc[:, ls]) + cq) - ck_cache[h:h + 1, :]
        s2 = (_dot_nt(qh, kn[:, ls]) + cq) - ck_new[_CUM_LANE + h:_CUM_LANE + h + 1, :]
        s2 = jnp.where(causal, s2, NEG_BIG)
        m = jnp.maximum(jnp.max(s1, axis=1, keepdims=True), jnp.max(s2, axis=1, keepdims=True))
        p1 = jnp.exp(s1 - m)
        p2 = jnp.exp(s2 - m)
        l = jnp.sum(p1, axis=1, keepdims=True) + jnp.sum(p2, axis=1, keepdims=True)
        outs.append((_dot(p1.astype(BF16), vc[:, ls]) + _dot(p2.astype(BF16), vn[:, ls])) / l)
    for p in range(FOX_HEADS // 2):
        o_ref[0, :, LANES * p:LANES * (p + 1)] = jnp.where(lane < FOX_DH, outs[2 * p], outs[2 * p + 1]).astype(BF16)


def _fox_sample(q, kb, vb, gates, past_k, past_v, past_logf):
    bsz, t, _ = kb.shape
    past = past_k.shape[1]
    blk = lambda rows, width: pl.BlockSpec((1, rows, width), lambda b: (b, 0, 0))
    return pl.pallas_call(
        _fox_sample_kernel,
        grid=(bsz,),
        in_specs=[blk(t, 2 * FOX_W), blk(t, FOX_W), blk(t, FOX_W), blk(t, LANES),
                  blk(past, FOX_W), blk(past, FOX_W), blk(past, LANES)],
        out_specs=blk(t, FOX_W),
        out_shape=jax.ShapeDtypeStruct((bsz, t, FOX_W), BF16),
        compiler_params=_params(("parallel",)),
        name="fox_sample",
    )(q, kb, vb, gates, past_k, past_v, past_logf)


def _unit_lower_inverse(nmat, eye):
    size = nmat.shape[0]
    inv = eye - nmat
    power = nmat
    p = 1
    while 2 * p < size:
        pb = power.astype(BF16)
        power = _dot(pb, pb)
        inv = inv + _dot(inv.astype(BF16), power.astype(BF16))
        p *= 2
    return inv


def _gdn_kernel(x_ref, gates_ref, z_ref, buf_ref, cw_ref, s0_ref, ng_ref,
                o_ref, sfin_ref, cbuf_ref, xbuf, s_scr, *, chunk):
    t = pl.program_id(1)
    tc = x_ref.shape[1]
    pad = SUBLANES

    @pl.when(t == 0)
    def _():
        xbuf[0:pad, :] = jnp.zeros((pad, GDN_CONV_DIM), F32)
        xbuf[pad - (CONV_W - 1):pad, :] = buf_ref[0]
        s_scr[...] = s0_ref[0]

    @pl.when(t > 0)
    def _():
        xbuf[0:pad, :] = xbuf[tc:tc + pad, :]

    xbuf[pad:pad + tc, :] = x_ref[0]
    cw = cw_ref[...]
    conv = cw[0:1] * xbuf[pad - 3:pad - 3 + tc, :]
    for j in range(1, CONV_W):
        conv = conv + cw[j:j + 1] * xbuf[pad - 3 + j:pad - 3 + j + tc, :]
    cbuf_ref[0] = xbuf[pad + tc - (CONV_W - 1):pad + tc, :]
    act = jax.nn.silu(conv)

    gts = gates_ref[0]
    shift = chunk.bit_length() - 1
    row = lax.broadcasted_iota(jnp.int32, (tc, tc), 0)
    col = lax.broadcasted_iota(jnp.int32, (tc, tc), 1)
    tri = jnp.where((row >= col) & ((row >> shift) == (col >> shift)), 1.0, 0.0).astype(BF16)
    gcum = _tri_cumsum(tri, gts)
    gcum_t = gcum.T
    r_c = lax.broadcasted_iota(jnp.int32, (chunk, chunk), 0)
    c_c = lax.broadcasted_iota(jnp.int32, (chunk, chunk), 1)
    lower = r_c >= c_c
    strict = r_c > c_c
    eye = jnp.where(r_c == c_c, 1.0, 0.0)
    ng = ng_ref[...]

    for h in range(GDN_HEADS):
        hs = slice(GDN_DK * h, GDN_DK * (h + 1))
        q = act[:, hs]
        k = act[:, GDN_QK + GDN_DK * h:GDN_QK + GDN_DK * (h + 1)]
        v = act[:, 2 * GDN_QK + GDN_DV * h:2 * GDN_QK + GDN_DV * (h + 1)]
        qn = q * lax.rsqrt(jnp.sum(q * q, -1, keepdims=True) + NORM_EPS) * (GDN_DK ** -0.5)
        kn = k * lax.rsqrt(jnp.sum(k * k, -1, keepdims=True) + NORM_EPS)
        beta = gts[:, _BETA_LANE + h:_BETA_LANE + h + 1]
        gc = gcum[:, _G_LANE + h:_G_LANE + h + 1]
        eg = jnp.exp(gc)
        kbeta = kn * beta
        rhs = jnp.concatenate([v * beta, kbeta * eg], axis=1).astype(BF16)
        qg = (qn * eg).astype(BF16)
        kn_b = kn.astype(BF16)
        kbeta_b = kbeta.astype(BF16)
        qn_b = qn.astype(BF16)
        for c in range(tc // chunk):
            sl = slice(c * chunk, (c + 1) * chunk)
            gcc = gc[sl]
            gcr = gcum_t[_G_LANE + h:_G_LANE + h + 1, sl]
            decay = jnp.where(lower, jnp.exp(jnp.where(lower, gcc - gcr, 0.0)), 0.0)
            nmat = jnp.where(strict, _dot_nt(kbeta_b[sl], kn_b[sl]) * decay, 0.0)
            tmat = _unit_lower_inverse(nmat, eye)
            uw = _dot(tmat.astype(BF16), rhs[sl])
            u = uw[:, :GDN_DV]
            w = uw[:, GDN_DV:]
            attn = _dot_nt(qn_b[sl], kn_b[sl]) * decay
            g_last = gcc[chunk - 1:chunk]
            kdec = kn[sl] * jnp.exp(g_last - gcc)
            state = s_scr[h]
            ws = _dot(jnp.concatenate([w.astype(BF16), qg[sl]], axis=0), state.astype(BF16))
            v_new = u - ws[:chunk]
            o = ws[chunk:] + _dot(attn.astype(BF16), v_new.astype(BF16))
            s_scr[h] = state * jnp.exp(g_last) + _dot(kdec.T.astype(BF16), v_new.astype(BF16))
            z = z_ref[0, sl, hs]
            o = o * lax.rsqrt(jnp.mean(o * o, -1, keepdims=True) + NORM_EPS) * ng * jax.nn.silu(z)
            o_ref[0, sl, hs] = o.astype(BF16)

    sfin_ref[0] = s_scr[...]


def _gdn(gx, gates, gz, buf, conv_w, s0, norm_g):
    bsz, t, _ = gx.shape
    chunk = min(t, CHUNK)
    tc = min(GDN_ROWS, t)
    row_spec = lambda width: pl.BlockSpec((1, tc, width), lambda b, i: (b, i, 0))
    return pl.pallas_call(
        functools.partial(_gdn_kernel, chunk=chunk),
        grid=(bsz, t // tc),
        in_specs=[row_spec(GDN_CONV_DIM), row_spec(LANES), row_spec(GDN_VW),
                  pl.BlockSpec((1, CONV_W - 1, GDN_CONV_DIM), lambda b, i: (b, 0, 0)),
                  _const_spec((CONV_W, GDN_CONV_DIM)),
                  pl.BlockSpec((1, GDN_HEADS, GDN_DK, GDN_DV), lambda b, i: (b, 0, 0, 0)),
                  _const_spec((1, GDN_DV))],
        out_specs=[row_spec(GDN_VW),
                   pl.BlockSpec((1, GDN_HEADS, GDN_DK, GDN_DV), lambda b, i: (b, 0, 0, 0)),
                   pl.BlockSpec((1, CONV_W - 1, GDN_CONV_DIM), lambda b, i: (b, 0, 0))],
        out_shape=[jax.ShapeDtypeStruct((bsz, t, GDN_VW), BF16),
                   jax.ShapeDtypeStruct((bsz, GDN_HEADS, GDN_DK, GDN_DV), F32),
                   jax.ShapeDtypeStruct((bsz, CONV_W - 1, GDN_CONV_DIM), F32)],
        scratch_shapes=[pltpu.VMEM((tc + SUBLANES, GDN_CONV_DIM), F32),
                        pltpu.VMEM((GDN_HEADS, GDN_DK, GDN_DV), F32)],
        compiler_params=_params(("parallel", "arbitrary")),
        name="gdn",
    )(gx, gates, gz, buf, conv_w, s0, norm_g)


def _even_out_kernel(x_ref, of_ref, og_ref, w_ref, g_ref, b_ref, o_ref):
    y = _dot(of_ref[...], w_ref[0:FOX_W, :]) + _dot(og_ref[...], w_ref[FOX_W:FOX_W + GDN_VW, :])
    o_ref[...] = _layer_norm(ALPHA * x_ref[...] + y, g_ref[...], b_ref[...])


def _even_out(x, o_fox, o_gdn, w, g, b):
    n = x.shape[0]
    tm = min(FFN_ROWS, n)
    return pl.pallas_call(
        _even_out_kernel,
        grid=(n // tm,),
        in_specs=[pl.BlockSpec((tm, D_MODEL), lambda i: (i, 0)),
                  pl.BlockSpec((tm, FOX_W), lambda i: (i, 0)),
                  pl.BlockSpec((tm, GDN_VW), lambda i: (i, 0)),
                  _const_spec((FOX_W + GDN_VW, D_MODEL)), _const_spec((1, D_MODEL)), _const_spec((1, D_MODEL))],
        out_specs=pl.BlockSpec((tm, D_MODEL), lambda i: (i, 0)),
        out_shape=jax.ShapeDtypeStruct((n, D_MODEL), F32),
        compiler_params=_params(("parallel",)),
        name="even_out",
    )(x, o_fox, o_gdn, w, g, b)


def _odd_kernel(x_ref, h0r_ref, h0i_ref, win_ref, bre_ref, bim_ref, cre_ref, cim_ref, sw_ref, d_ref,
                gw_ref, gb_ref, wout_ref, g_ref, b_ref,
                o_ref, hr_out, hi_out, sre, sim, hr_scr, hi_scr):
    t = pl.program_id(1)
    tm = x_ref.shape[1]
    nblk = D_MODEL // MXU_DIM
    per = S5_LANES // nblk

    @pl.when(t == 0)
    def _():
        hr_scr[...] = h0r_ref[0]
        hi_scr[...] = h0i_ref[0]

    x = x_ref[0]
    u = _dot(x.astype(BF16), win_ref[...])
    ub = u.astype(BF16)
    for kk in range(nblk):
        uk = ub[:, MXU_DIM * kk:MXU_DIM * (kk + 1)]
        sre[:, per * kk:per * (kk + 1)] = _dot(uk, bre_ref[kk])
        sim[:, per * kk:per * (kk + 1)] = _dot(uk, bim_ref[kk])

    ncol = S5_LANES // S5_COLS

    def slab(r, carry):
        rows = pl.ds(pl.multiple_of(r * SUBLANES, SUBLANES), SUBLANES)
        new = []
        for cb in range(ncol):
            cs = slice(S5_COLS * cb, S5_COLS * (cb + 1))
            hr, hi = carry[cb]
            xr = sre[rows, cs]
            xi = sim[rows, cs]
            for idx in range(3):
                wr = sw_ref[2 * idx, :, cs]
                wi = sw_ref[2 * idx + 1, :, cs]
                sr = pltpu.roll(xr, 1 << idx, 0)
                si = pltpu.roll(xi, 1 << idx, 0)
                xr, xi = xr + (wr * sr - wi * si), xi + (wr * si + wi * sr)
            pr = sw_ref[6, :, cs]
            pi = sw_ref[7, :, cs]
            xr, xi = xr + (pr * hr - pi * hi), xi + (pr * hi + pi * hr)
            sre[rows, cs] = xr
            sim[rows, cs] = xi
            new.append((xr[SUBLANES - 1:SUBLANES], xi[SUBLANES - 1:SUBLANES]))
        return tuple(new)

    init = tuple((hr_scr[:, S5_COLS * cb:S5_COLS * (cb + 1)], hi_scr[:, S5_COLS * cb:S5_COLS * (cb + 1)])
                 for cb in range(ncol))
    fin = lax.fori_loop(0, tm // SUBLANES, slab, init)
    for cb in range(ncol):
        hr_scr[:, S5_COLS * cb:S5_COLS * (cb + 1)] = fin[cb][0]
        hi_scr[:, S5_COLS * cb:S5_COLS * (cb + 1)] = fin[cb][1]
    hr_out[0] = hr_scr[...]
    hi_out[0] = hi_scr[...]

    ys = []
    for kk in range(nblk):
        s_re = sre[:, per * kk:per * (kk + 1)].astype(BF16)
        s_im = sim[:, per * kk:per * (kk + 1)].astype(BF16)
        ys.append(_dot(s_re, cre_ref[kk]) - _dot(s_im, cim_ref[kk]))
    y = jnp.concatenate(ys, axis=1) + d_ref[...] * u
    zz = jax.nn.gelu(y)
    gated = zz * jax.nn.sigmoid(_dot(zz.astype(BF16), gw_ref[...]) + gb_ref[...])
    out = _dot(gated.astype(BF16), wout_ref[...])
    o_ref[0] = _layer_norm(ALPHA * x + out, g_ref[...], b_ref[...])


def _odd(x, h0r, h0i, w_in, bre, bim, cre, cim, sw, d, gw, gb, w_out, g, b):
    bsz, t, _ = x.shape
    tm = min(ODD_ROWS, t)
    nblk = D_MODEL // MXU_DIM
    per = S5_LANES // nblk
    st_spec = pl.BlockSpec((1, 1, S5_LANES), lambda bb, i: (bb, 0, 0))
    return pl.pallas_call(
        _odd_kernel,
        grid=(bsz, t // tm),
        in_specs=[pl.BlockSpec((1, tm, D_MODEL), lambda bb, i: (bb, i, 0)), st_spec, st_spec,
                  _const_spec((D_MODEL, D_MODEL)),
                  _const_spec((nblk, MXU_DIM, per)), _const_spec((nblk, MXU_DIM, per)),
                  _const_spec((nblk, per, MXU_DIM)), _const_spec((nblk, per, MXU_DIM)),
                  _const_spec((8, SUBLANES, S5_LANES)), _const_spec((1, D_MODEL)),
                  _const_spec((D_MODEL, D_MODEL)), _const_spec((1, D_MODEL)), _const_spec((D_MODEL, D_MODEL)),
                  _const_spec((1, D_MODEL)), _const_spec((1, D_MODEL))],
        out_specs=[pl.BlockSpec((1, tm, D_MODEL), lambda bb, i: (bb, i, 0)), st_spec, st_spec],
        out_shape=[jax.ShapeDtypeStruct((bsz, t, D_MODEL), F32),
                   jax.ShapeDtypeStruct((bsz, 1, S5_LANES), F32),
                   jax.ShapeDtypeStruct((bsz, 1, S5_LANES), F32)],
        scratch_shapes=[pltpu.VMEM((tm, S5_LANES), F32), pltpu.VMEM((tm, S5_LANES), F32),
                        pltpu.VMEM((1, S5_LANES), F32), pltpu.VMEM((1, S5_LANES), F32)],
        compiler_params=_params(("parallel", "arbitrary")),
        name="odd_s5",
    )(x, h0r, h0i, w_in, bre, bim, cre, cim, sw, d, gw, gb, w_out, g, b)


def _prep_even(w_in, b_f, a_log, dt_bias):
    offs = [0]
    for s in EVEN_SPLITS:
        offs.append(offs[-1] + s)
    fq, fk, fv, ff, gx, ga, gb, gz = [w_in[:, offs[i]:offs[i + 1]] for i in range(len(EVEN_SPLITS))]
    pad = jnp.zeros((D_MODEL, LANES - (2 * FOX_HEADS + 2 * GDN_HEADS)), w_in.dtype)
    w = jnp.concatenate([fq, fk, fv, gx, gz, ff, ga, gb, ff, pad], axis=1).astype(BF16)
    zpad = jnp.zeros((LANES - (2 * FOX_HEADS + 2 * GDN_HEADS),), F32)
    bias = jnp.concatenate([b_f.astype(F32), dt_bias.astype(F32), jnp.zeros((GDN_HEADS,), F32), b_f.astype(F32), zpad])
    alog = jnp.concatenate([jnp.zeros((FOX_HEADS,), F32), a_log.astype(F32),
                            jnp.zeros((LANES - FOX_HEADS - GDN_HEADS,), F32)])
    return w, bias.reshape(1, LANES), alog.reshape(1, LANES)


def _block_diag(blocks):
    nblk, g, r, c = blocks.shape
    eye = jnp.eye(g, dtype=blocks.dtype)
    return jnp.einsum('kgrc,gh->kgrhc', blocks, eye).reshape(nblk, g * r, g * c)


def _prep_odd(lam_re, lam_im, log_step, b_re, b_im, c_re, c_im):
    lam = lax.complex(lam_re.astype(F32), lam_im.astype(F32))
    lam_bar = jnp.exp(lam * jnp.exp(log_step.astype(F32))[:, None])
    b_bar = ((lam_bar - 1.0) / lam)[..., None] * lax.complex(b_re.astype(F32), b_im.astype(F32))
    nblk = D_MODEL // MXU_DIM
    gper = S5_GROUPS // nblk
    bt = jnp.swapaxes(b_bar, 1, 2).reshape(nblk, gper, S5_GROUP, S5_STATE)
    bre = _block_diag(jnp.real(bt)).astype(BF16)
    bim = _block_diag(jnp.imag(bt)).astype(BF16)
    ct_re = jnp.swapaxes(c_re.astype(F32), 1, 2).reshape(nblk, gper, S5_STATE, S5_GROUP)
    ct_im = jnp.swapaxes(c_im.astype(F32), 1, 2).reshape(nblk, gper, S5_STATE, S5_GROUP)
    cre = _block_diag(ct_re).astype(BF16)
    cim = _block_diag(ct_im).astype(BF16)
    lam1 = lam_bar.reshape(1, S5_LANES)
    lam2 = lam1 * lam1
    lam4 = lam2 * lam2
    rows = jnp.arange(SUBLANES)[:, None]
    pw = [lam1]
    for _ in range(SUBLANES - 1):
        pw.append(pw[-1] * lam1)
    pw = jnp.concatenate(pw, axis=0)
    parts = []
    for sh, lm in ((1, lam1), (2, lam2), (4, lam4)):
        wgt = jnp.where(rows >= sh, jnp.broadcast_to(lm, (SUBLANES, S5_LANES)), 0.0)
        parts += [jnp.real(wgt), jnp.imag(wgt)]
    parts += [jnp.real(pw), jnp.imag(pw)]
    return bre, bim, cre, cim, jnp.stack(parts).astype(F32)


def _even_layer(x, fox_past, s0, buf0, w_in, bias, alog, conv_w, norm_g, w_out, g, b):
    bsz, t, _ = x.shape
    q, k, v, kb, vb, gx, gz, gates, ct = _even_in(x, w_in, bias, alog)
    if fox_past is None:
        o_fox = _fox_prompt(q, kb, vb, gates, ct)
    else:
        o_fox = _fox_sample(q, kb, vb, gates, *fox_past)
    o_gdn, s_new, new_buf = _gdn(gx, gates, gz, buf0, conv_w, s0, norm_g)
    y = _even_out(x.reshape(bsz * t, D_MODEL), o_fox.reshape(bsz * t, FOX_W), o_gdn.reshape(bsz * t, GDN_VW), w_out, g, b)
    state = (k.reshape(bsz, t, FOX_HEADS, FOX_DH), v.reshape(bsz, t, FOX_HEADS, FOX_DH),
             gates[:, :, :FOX_HEADS], s_new, new_buf)
    return y.reshape(bsz, t, D_MODEL), state


def kernel(x_prompt, x_sample, cache_fox_k, cache_fox_v, cache_fox_logf, state_gdn, state_gdn_conv, state_s5_re, state_s5_im, ffn_w_in, ffn_w_out, ln_g, ln_b, even_w_in, fox_b_f, gdn_conv_w, gdn_a_log, gdn_dt_bias, gdn_norm_g, even_w_out, odd_w_in, s5_lam_re, s5_lam_im, s5_log_step, s5_b_re, s5_b_im, s5_c_re, s5_c_im, s5_d, s5_glu_w, s5_glu_b, odd_w_out):
    bp, tp, _ = x_prompt.shape
    bs, ts, _ = x_sample.shape
    past = cache_fox_k.shape[2]
    xp, xs = x_prompt, x_sample
    even_p, even_s, odd_p, odd_s = [], [], [], []
    row = lambda a: a.astype(F32).reshape(1, -1)

    def ffn_pair(xp, xs, layer, idx, ln_idx):
        w_in = ffn_w_in[layer, idx].astype(BF16)
        w_out = ffn_w_out[layer, idx].astype(BF16)
        g, b = row(ln_g[layer, ln_idx]), row(ln_b[layer, ln_idx])
        xp = _ffn(xp.reshape(bp * tp, D_MODEL), w_in, w_out, g, b).reshape(bp, tp, D_MODEL)
        xs = _ffn(xs.reshape(bs * ts, D_MODEL), w_in, w_out, g, b).reshape(bs, ts, D_MODEL)
        return xp, xs

    for layer in range(DEPTH):
        xp, xs = ffn_pair(xp, xs, layer, 0, 0)
        j = layer // 2
        g1, b1 = row(ln_g[layer, 1]), row(ln_b[layer, 1])
        if layer % 2 == 0:
            w_in, bias, alog = _prep_even(even_w_in[j], fox_b_f[j], gdn_a_log[j], gdn_dt_bias[j])
            common = (w_in, bias, alog, gdn_conv_w[j].astype(F32), row(gdn_norm_g[j]), even_w_out[j].astype(BF16), g1, b1)
            s0 = jnp.zeros((bp, GDN_HEADS, GDN_DK, GDN_DV), F32)
            buf0 = jnp.zeros((bp, CONV_W - 1, GDN_CONV_DIM), F32)
            xp, st_p = _even_layer(xp, None, s0, buf0, *common)
            lc = jnp.pad(cache_fox_logf[j].astype(F32), ((0, 0), (0, 0), (0, LANES - FOX_HEADS)))
            fox_past = (cache_fox_k[j].reshape(bs, past, FOX_W), cache_fox_v[j].reshape(bs, past, FOX_W), lc)
            xs, st_s = _even_layer(xs, fox_past, state_gdn[j].astype(F32), state_gdn_conv[j].astype(F32), *common)
            even_p.append(st_p)
            even_s.append(st_s)
        else:
            bre, bim, cre, cim, sw = _prep_odd(s5_lam_re[j], s5_lam_im[j], s5_log_step[j], s5_b_re[j], s5_b_im[j],
                                               s5_c_re[j], s5_c_im[j])
            common = (odd_w_in[j].astype(BF16), bre, bim, cre, cim, sw, row(s5_d[j]), s5_glu_w[j].astype(BF16),
                      row(s5_glu_b[j]), odd_w_out[j].astype(BF16), g1, b1)
            h0 = jnp.zeros((bp, 1, S5_LANES), F32)
            xp, hr, hi = _odd(xp, h0, h0, *common)
            odd_p.append((hr.reshape(bp, S5_GROUPS, S5_STATE), hi.reshape(bp, S5_GROUPS, S5_STATE)))
            xs, hr, hi = _odd(xs, state_s5_re[j].astype(F32).reshape(bs, 1, S5_LANES),
                              state_s5_im[j].astype(F32).reshape(bs, 1, S5_LANES), *common)
            odd_s.append((hr.reshape(bs, S5_GROUPS, S5_STATE), hi.reshape(bs, S5_GROUPS, S5_STATE)))
        xp, xs = ffn_pair(xp, xs, layer, 1, 2)

    stack = lambda states, idx: jnp.stack([s[idx] for s in states])
    return (xp, xs,
            stack(even_p, 0), stack(even_p, 1), stack(even_p, 2), stack(even_p, 3), stack(even_p, 4),
            stack(odd_p, 0), stack(odd_p, 1),
            stack(even_s, 0), stack(even_s, 1), stack(even_s, 2), stack(even_s, 3), stack(even_s, 4),
            stack(odd_s, 0), stack(odd_s, 1))
```

```python
import functools

import jax
import jax.numpy as jnp
from jax import lax
from jax.experimental import pallas as pl
from jax.experimental.pallas import tpu as pltpu

F32 = jnp.float32
BF16 = jnp.bfloat16

D_MODEL = 1024
DEPTH = 2
CHUNK = 64
FOX_HEADS = 8
FOX_DH = 64
FOX_W = FOX_HEADS * FOX_DH
GDN_HEADS = 4
GDN_DK = 128
GDN_DV = 128
GDN_QK = GDN_HEADS * GDN_DK
GDN_VW = GDN_HEADS * GDN_DV
GDN_CONV_DIM = 2 * GDN_QK + GDN_VW
CONV_W = 4
S5_GROUP = 16
S5_GROUPS = D_MODEL // S5_GROUP
S5_STATE = 64
S5_LANES = S5_GROUPS * S5_STATE
D_FF = ((8 * D_MODEL // 3 + 127) // 128) * 128
ALPHA = (2.0 * DEPTH) ** 0.25
LN_EPS = 1e-5
NORM_EPS = 1e-6
EVEN_SPLITS = (FOX_W, FOX_W, FOX_W, FOX_HEADS, GDN_CONV_DIM, GDN_HEADS, GDN_HEADS, GDN_VW)

LANES = 128
SUBLANES = 8
MXU_DIM = 256
VMEM_LIMIT = 56 * 1024 * 1024

FOX_XW = FOX_HEADS * LANES
_Q0 = 0
_KX0 = _Q0 + FOX_XW
_K0 = _KX0 + FOX_XW
_V0 = _K0 + FOX_W
_X0 = _V0 + FOX_W
_Z0 = _X0 + GDN_CONV_DIM
_G0 = _Z0 + GDN_VW
EVEN_COLS = _G0 + LANES
_BIAS_LANE = FOX_DH
_ONES_ROW = FOX_DH
_LOGF_LANE, _G_LANE, _BETA_LANE, _CUM_LANE = 0, FOX_HEADS, FOX_HEADS + GDN_HEADS, FOX_HEADS + 2 * GDN_HEADS
NEG_BIG = -1e30

FFN_ROWS = 512
FFN_COLS = 256
EVEN_ROWS = 512
ATT_ROWS = 512
ATT_HEADS = 2
GDN_ROWS = 256
ODD_ROWS = 256
S5_COLS = 1024


def _dot(a, b):
    return jnp.dot(a, b, preferred_element_type=F32)


def _dot_nt(a, b):
    return lax.dot_general(a, b, (((1,), (1,)), ((), ())), preferred_element_type=F32)


def _layer_norm(z, g, b):
    mu = jnp.mean(z, -1, keepdims=True)
    d = z - mu
    var = jnp.mean(d * d, -1, keepdims=True)
    return d * lax.rsqrt(var + LN_EPS) * g + b


def _split3(x):
    hi = x.astype(BF16)
    r = x - hi.astype(F32)
    mid = r.astype(BF16)
    lo = (r - mid.astype(F32)).astype(BF16)
    return hi, mid, lo


def _tri_cumsum(tri, x):
    hi, mid, lo = _split3(x)
    return _dot(tri, hi) + _dot(tri, mid) + _dot(tri, lo)


def _const_spec(shape):
    nd = len(shape)
    return pl.BlockSpec(shape, lambda *_: (0,) * nd, pipeline_mode=pl.Buffered(1))


def _params(sem):
    return pltpu.CompilerParams(dimension_semantics=sem, vmem_limit_bytes=VMEM_LIMIT)


def _ffn_kernel(x_ref, win_ref, wout_ref, g_ref, b_ref, o_ref, h_ref):
    x = x_ref[...]
    xb = x.astype(BF16)
    for c in range(D_FF // FFN_COLS):
        lo = c * FFN_COLS
        a = _dot(xb, win_ref[:, lo:lo + FFN_COLS])
        b = _dot(xb, win_ref[:, D_FF + lo:D_FF + lo + FFN_COLS])
        h_ref[:, lo:lo + FFN_COLS] = (jax.nn.silu(a) * b).astype(BF16)
    y = _dot(h_ref[...], wout_ref[...])
    o_ref[...] = _layer_norm(ALPHA * x + 0.5 * y, g_ref[...], b_ref[...])


def _ffn(x, w_in, w_out, g, b):
    n = x.shape[0]
    tm = min(FFN_ROWS, n)
    return pl.pallas_call(
        _ffn_kernel,
        grid=(n // tm,),
        in_specs=[pl.BlockSpec((tm, D_MODEL), lambda i: (i, 0)),
                  _const_spec((D_MODEL, 2 * D_FF)),
                  _const_spec((D_FF, D_MODEL)),
                  _const_spec((1, D_MODEL)),
                  _const_spec((1, D_MODEL))],
        out_specs=pl.BlockSpec((tm, D_MODEL), lambda i: (i, 0)),
        out_shape=jax.ShapeDtypeStruct((n, D_MODEL), F32),
        scratch_shapes=[pltpu.VMEM((tm, D_FF), BF16)],
        compiler_params=_params(("parallel",)),
        name="ffn_ln",
    )(x, w_in, w_out, g, b)


def _even_in_kernel(x_ref, w_ref, bias_ref, alog_ref,
                    q_ref, kx_ref, vt_ref, k_ref, v_ref, gx_ref, gz_ref, gates_ref,
                    carry_ref):
    t = pl.program_id(1)
    tm = x_ref.shape[1]

    @pl.when(t == 0)
    def _():
        carry_ref[...] = jnp.zeros_like(carry_ref)

    xb = x_ref[0].astype(BF16)

    def mm(lo, width):
        return _dot(xb, w_ref[:, lo:lo + width])

    lane = lax.broadcasted_iota(jnp.int32, (tm, LANES), 1)
    raw = mm(_G0, LANES) + bias_ref[...]
    logf = jax.nn.log_sigmoid(raw)
    gdec = -jnp.exp(alog_ref[...]) * jax.nn.softplus(raw)
    beta = jax.nn.sigmoid(raw)
    gates = jnp.where(lane < _G_LANE, logf, jnp.where(lane < _BETA_LANE, gdec, jnp.where(lane < _CUM_LANE, beta, logf)))
    row = lax.broadcasted_iota(jnp.int32, (tm, tm), 0)
    col = lax.broadcasted_iota(jnp.int32, (tm, tm), 1)
    tri = jnp.where(row >= col, 1.0, 0.0).astype(BF16)
    cum = _tri_cumsum(tri, logf) + carry_ref[...]
    carry_ref[...] = cum[tm - 1:tm, :]
    gates_ref[0] = jnp.where((lane >= _CUM_LANE) & (lane < _CUM_LANE + FOX_HEADS), cum, gates)

    k = mm(_K0, FOX_W)
    k_ref[0] = k
    v = mm(_V0, FOX_W)
    v_ref[0] = v
    gx_ref[0] = mm(_X0, GDN_CONV_DIM)
    gz_ref[0] = mm(_Z0, GDN_VW)

    c_parts = [part.astype(F32) for part in _split3(cum)]
    qx = mm(_Q0, FOX_XW) * (FOX_DH ** -0.5)
    kx = mm(_KX0, FOX_XW)
    for h in range(FOX_HEADS):
        hs = slice(LANES * h, LANES * (h + 1))
        ch = [part[:, _CUM_LANE + h:_CUM_LANE + h + 1] for part in c_parts]
        q_ext = jnp.where(lane < _BIAS_LANE + 3, 0.0, jnp.where(lane < _BIAS_LANE + 6, 1.0, 0.0))
        k_ext = jnp.where(lane < _BIAS_LANE + 3, 1.0, 0.0)
        for i in range(3):
            q_ext = jnp.where(lane == _BIAS_LANE + i, ch[i], q_ext)
            k_ext = jnp.where(lane == _BIAS_LANE + 3 + i, -ch[i], k_ext)
        q_ref[0, :, hs] = jnp.where(lane < FOX_DH, qx[:, hs], q_ext).astype(BF16)
        kx_ref[0, :, hs] = jnp.where(lane < FOX_DH, kx[:, hs], k_ext).astype(BF16)

    v_t = v.T
    pack = 2 * SUBLANES
    ones_blk = jnp.where(lax.broadcasted_iota(jnp.int32, (pack, tm), 0) == 0, 1.0, 0.0).astype(BF16)
    for h in range(FOX_HEADS):
        base = LANES * h
        vt_ref[0, base:base + FOX_DH, :] = v_t[FOX_DH * h:FOX_DH * (h + 1), :].astype(BF16)
        vt_ref[0, base + _ONES_ROW:base + _ONES_ROW + pack, :] = ones_blk
        vt_ref[0, base + _ONES_ROW + pack:base + LANES, :] = jnp.zeros((LANES - _ONES_ROW - pack, tm), BF16)


def _even_in(x, w, bias, alog):
    bsz, t, _ = x.shape
    tm = min(EVEN_ROWS, t)
    row_spec = lambda width: pl.BlockSpec((1, tm, width), lambda b, i: (b, i, 0))
    shp = lambda width, dt: jax.ShapeDtypeStruct((bsz, t, width), dt)
    return pl.pallas_call(
        _even_in_kernel,
        grid=(bsz, t // tm),
        in_specs=[row_spec(D_MODEL), _const_spec((D_MODEL, EVEN_COLS)), _const_spec((1, LANES)), _const_spec((1, LANES))],
        out_specs=[row_spec(FOX_XW), row_spec(FOX_XW),
                   pl.BlockSpec((1, FOX_XW, tm), lambda b, i: (b, 0, i)),
                   row_spec(FOX_W), row_spec(FOX_W), row_spec(GDN_CONV_DIM), row_spec(GDN_VW), row_spec(LANES)],
        out_shape=[shp(FOX_XW, BF16), shp(FOX_XW, BF16), jax.ShapeDtypeStruct((bsz, FOX_XW, t), BF16),
                   shp(FOX_W, F32), shp(FOX_W, F32), shp(GDN_CONV_DIM, F32), shp(GDN_VW, F32), shp(LANES, F32)],
        scratch_shapes=[pltpu.VMEM((1, LANES), F32)],
        compiler_params=_params(("parallel", "arbitrary")),
        name="even_in",
    )(x, w, bias, alog)


def _fox_prompt_kernel(q_ref, k_ref, vt_ref, o_ref, m_ref, acc_ref):
    qi = pl.program_id(2)
    tq = q_ref.shape[1]
    nh = q_ref.shape[2] // LANES
    m_ref[...] = jnp.full(m_ref.shape, NEG_BIG, F32)
    acc_ref[...] = jnp.zeros_like(acc_ref)
    key_pos = lax.broadcasted_iota(jnp.int32, (tq, tq), 0)
    qry_pos = lax.broadcasted_iota(jnp.int32, (tq, tq), 1)

    def tile(j, masked):
        off = pl.multiple_of(j * tq, tq)
        for a in range(nh):
            hs = slice(LANES * a, LANES * (a + 1))
            s_t = _dot_nt(k_ref[0, pl.ds(off, tq), hs], q_ref[0, :, hs])
            if masked:
                s_t = jnp.where(key_pos <= qry_pos, s_t, NEG_BIG)
            m_prev = m_ref[a]
            m_new = jnp.maximum(m_prev, jnp.max(s_t, axis=0, keepdims=True))
            p_t = jnp.exp(s_t - m_new).astype(BF16)
            acc_ref[a] = jnp.exp(m_prev - m_new) * acc_ref[a] + _dot(vt_ref[0, hs, pl.ds(off, tq)], p_t)
            m_ref[a] = m_new

    def body(j, c):
        tile(j, False)
        return c

    lax.fori_loop(0, qi, body, 0)
    tile(qi, True)
    for a in range(nh):
        acc = acc_ref[a]
        o_ref[0, FOX_DH * a:FOX_DH * (a + 1), :] = (acc[0:FOX_DH] / acc[_ONES_ROW:_ONES_ROW + 1]).astype(BF16)


def _fox_prompt(q, kx, vt):
    bsz, t, _ = q.shape
    tq = min(ATT_ROWS, t)
    nh = ATT_HEADS
    return pl.pallas_call(
        _fox_prompt_kernel,
        grid=(bsz, FOX_HEADS // nh, t // tq),
        in_specs=[pl.BlockSpec((1, tq, nh * LANES), lambda b, p, i: (b, i, p)),
                  pl.BlockSpec((1, t, nh * LANES), lambda b, p, i: (b, 0, p)),
                  pl.BlockSpec((1, nh * LANES, t), lambda b, p, i: (b, p, 0))],
        out_specs=pl.BlockSpec((1, nh * FOX_DH, tq), lambda b, p, i: (b, p, i)),
        out_shape=jax.ShapeDtypeStruct((bsz, FOX_W, t), BF16),
        scratch_shapes=[pltpu.VMEM((nh, 1, tq), F32), pltpu.VMEM((nh, LANES, tq), F32)],
        compiler_params=_params(("parallel", "parallel", "arbitrary")),
        name="fox_prompt",
    )(q, kx, vt)


def _fox_sample_kernel(q_ref, kn_ref, vn_ref, gates_ref, kc_ref, vc_ref, lc_ref, o_ref):
    past = kc_ref.shape[1]
    t = q_ref.shape[1]
    row = lax.broadcasted_iota(jnp.int32, (past, past), 0)
    col = lax.broadcasted_iota(jnp.int32, (past, past), 1)
    tri = jnp.where(row >= col, 1.0, 0.0).astype(BF16)
    cum = _tri_cumsum(tri, lc_ref[0])
    ck_cache = (cum - cum[past - 1:past, :]).T
    gts = gates_ref[0]
    ck_new = gts.T
    kc = kc_ref[0].astype(BF16)
    vc = vc_ref[0].astype(BF16)
    kn = kn_ref[0].astype(BF16)
    vn = vn_ref[0].astype(BF16)
    causal = lax.broadcasted_iota(jnp.int32, (t, t), 1) <= lax.broadcasted_iota(jnp.int32, (t, t), 0)
    lane = lax.broadcasted_iota(jnp.int32, (t, LANES), 1)
    outs = []
    for h in range(FOX_HEADS):
        ls = slice(LANES * (h // 2), LANES * (h // 2 + 1))
        qh = q_ref[0, :, LANES * h:LANES * (h + 1)].astype(F32)
        if h % 2 == 0:
            qh = jnp.where(lane < FOX_DH, qh, 0.0).astype(BF16)
        else:
            qh = jnp.where(lane >= FOX_DH, pltpu.roll(qh, FOX_DH, 1), 0.0).astype(BF16)
        cq = gts[:, _CUM_LANE + h:_CUM_LANE + h + 1]
        s1 = (_dot_nt(qh, kc[:, ls]) + cq) - ck_cache[h:h + 1, :]
        s2 = (_dot_nt(qh, kn[:, ls]) + cq) - ck_new[_CUM_LANE + h:_CUM_LANE + h + 1, :]
        s2 = jnp.where(causal, s2, NEG_BIG)
        m = jnp.maximum(jnp.max(s1, axis=1, keepdims=True), jnp.max(s2, axis=1, keepdims=True))
        p1 = jnp.exp(s1 - m)
        p2 = jnp.exp(s2 - m)
        l = jnp.sum(p1, axis=1, keepdims=True) + jnp.sum(p2, axis=1, keepdims=True)
        outs.append((_dot(p1.astype(BF16), vc[:, ls]) + _dot(p2.astype(BF16), vn[:, ls])) / l)
    o = jnp.concatenate([jnp.where(lane < FOX_DH, outs[2 * p], outs[2 * p + 1]) for p in range(FOX_HEADS // 2)], axis=1)
    o_ref[0] = o.T.astype(BF16)


def _fox_sample(q, k, v, gates, past_k, past_v, past_logf):
    bsz, t, _ = k.shape
    past = past_k.shape[1]
    blk = lambda rows, width: pl.BlockSpec((1, rows, width), lambda b: (b, 0, 0))
    return pl.pallas_call(
        _fox_sample_kernel,
        grid=(bsz,),
        in_specs=[blk(t, FOX_XW), blk(t, FOX_W), blk(t, FOX_W), blk(t, LANES),
                  blk(past, FOX_W), blk(past, FOX_W), blk(past, LANES)],
        out_specs=blk(FOX_W, t),
        out_shape=jax.ShapeDtypeStruct((bsz, FOX_W, t), BF16),
        compiler_params=_params(("parallel",)),
        name="fox_sample",
    )(q, k, v, gates, past_k, past_v, past_logf)


def _unit_lower_inverse(nmat, eye):
    size = nmat.shape[0]
    inv = eye - nmat
    power = nmat
    p = 1
    while 2 * p < size:
        pb = power.astype(BF16)
        power = _dot(pb, pb)
        inv = inv + _dot(inv.astype(BF16), power.astype(BF16))
        p *= 2
    return inv


def _gdn_kernel(x_ref, gates_ref, z_ref, buf_ref, cw_ref, s0_ref, ng_ref,
                o_ref, sfin_ref, cbuf_ref, xbuf, s_scr, *, chunk):
    t = pl.program_id(1)
    tc = x_ref.shape[1]
    pad = SUBLANES

    @pl.when(t == 0)
    def _():
        xbuf[0:pad, :] = jnp.zeros((pad, GDN_CONV_DIM), F32)
        xbuf[pad - (CONV_W - 1):pad, :] = buf_ref[0]
        s_scr[...] = s0_ref[0]

    @pl.when(t > 0)
    def _():
        xbuf[0:pad, :] = xbuf[tc:tc + pad, :]

    xbuf[pad:pad + tc, :] = x_ref[0]
    cw = cw_ref[...]
    conv = cw[0:1] * xbuf[pad - 3:pad - 3 + tc, :]
    for j in range(1, CONV_W):
        conv = conv + cw[j:j + 1] * xbuf[pad - 3 + j:pad - 3 + j + tc, :]
    cbuf_ref[0] = xbuf[pad + tc - (CONV_W - 1):pad + tc, :]
    act = jax.nn.silu(conv)

    gts = gates_ref[0]
    shift = chunk.bit_length() - 1
    row = lax.broadcasted_iota(jnp.int32, (tc, tc), 0)
    col = lax.broadcasted_iota(jnp.int32, (tc, tc), 1)
    tri = jnp.where((row >= col) & ((row >> shift) == (col >> shift)), 1.0, 0.0).astype(BF16)
    gcum = _tri_cumsum(tri, gts)
    gcum_t = gcum.T
    r_c = lax.broadcasted_iota(jnp.int32, (chunk, chunk), 0)
    c_c = lax.broadcasted_iota(jnp.int32, (chunk, chunk), 1)
    lower = r_c >= c_c
    strict = r_c > c_c
    eye = jnp.where(r_c == c_c, 1.0, 0.0)
    ng = ng_ref[...]

    for h in range(GDN_HEADS):
        hs = slice(GDN_DK * h, GDN_DK * (h + 1))
        q = act[:, hs]
        k = act[:, GDN_QK + GDN_DK * h:GDN_QK + GDN_DK * (h + 1)]
        v = act[:, 2 * GDN_QK + GDN_DV * h:2 * GDN_QK + GDN_DV * (h + 1)]
        qn = q * lax.rsqrt(jnp.sum(q * q, -1, keepdims=True) + NORM_EPS) * (GDN_DK ** -0.5)
        kn = k * lax.rsqrt(jnp.sum(k * k, -1, keepdims=True) + NORM_EPS)
        beta = gts[:, _BETA_LANE + h:_BETA_LANE + h + 1]
        gc = gcum[:, _G_LANE + h:_G_LANE + h + 1]
        eg = jnp.exp(gc)
        kbeta = kn * beta
        rhs = jnp.concatenate([v * beta, kbeta * eg], axis=1).astype(BF16)
        qg = (qn * eg).astype(BF16)
        kn_b = kn.astype(BF16)
        kbeta_b = kbeta.astype(BF16)
        qn_b = qn.astype(BF16)
        for c in range(tc // chunk):
            sl = slice(c * chunk, (c + 1) * chunk)
            gcc = gc[sl]
            gcr = gcum_t[_G_LANE + h:_G_LANE + h + 1, sl]
            decay = jnp.where(lower, jnp.exp(jnp.where(lower, gcc - gcr, 0.0)), 0.0)
            nmat = jnp.where(strict, _dot_nt(kbeta_b[sl], kn_b[sl]) * decay, 0.0)
            tmat = _unit_lower_inverse(nmat, eye)
            uw = _dot(tmat.astype(BF16), rhs[sl])
            u = uw[:, :GDN_DV]
            w = uw[:, GDN_DV:]
            attn = _dot_nt(qn_b[sl], kn_b[sl]) * decay
            g_last = gcc[chunk - 1:chunk]
            kdec = kn[sl] * jnp.exp(g_last - gcc)
            state = s_scr[h]
            ws = _dot(jnp.concatenate([w.astype(BF16), qg[sl]], axis=0), state.astype(BF16))
            v_new = u - ws[:chunk]
            o = ws[chunk:] + _dot(attn.astype(BF16), v_new.astype(BF16))
            s_scr[h] = state * jnp.exp(g_last) + _dot(kdec.T.astype(BF16), v_new.astype(BF16))
            z = z_ref[0, sl, hs]
            o = o * lax.rsqrt(jnp.mean(o * o, -1, keepdims=True) + NORM_EPS) * ng * jax.nn.silu(z)
            o_ref[0, sl, hs] = o.astype(BF16)

    sfin_ref[0] = s_scr[...]


def _gdn(gx, gates, gz, buf, conv_w, s0, norm_g):
    bsz, t, _ = gx.shape
    chunk = min(t, CHUNK)
    tc = min(GDN_ROWS, t)
    row_spec = lambda width: pl.BlockSpec((1, tc, width), lambda b, i: (b, i, 0))
    return pl.pallas_call(
        functools.partial(_gdn_kernel, chunk=chunk),
        grid=(bsz, t // tc),
        in_specs=[row_spec(GDN_CONV_DIM), row_spec(LANES), row_spec(GDN_VW),
                  pl.BlockSpec((1, CONV_W - 1, GDN_CONV_DIM), lambda b, i: (b, 0, 0)),
                  _const_spec((CONV_W, GDN_CONV_DIM)),
                  pl.BlockSpec((1, GDN_HEADS, GDN_DK, GDN_DV), lambda b, i: (b, 0, 0, 0)),
                  _const_spec((1, GDN_DV))],
        out_specs=[row_spec(GDN_VW),
                   pl.BlockSpec((1, GDN_HEADS, GDN_DK, GDN_DV), lambda b, i: (b, 0, 0, 0)),
                   pl.BlockSpec((1, CONV_W - 1, GDN_CONV_DIM), lambda b, i: (b, 0, 0))],
        out_shape=[jax.ShapeDtypeStruct((bsz, t, GDN_VW), BF16),
                   jax.ShapeDtypeStruct((bsz, GDN_HEADS, GDN_DK, GDN_DV), F32),
                   jax.ShapeDtypeStruct((bsz, CONV_W - 1, GDN_CONV_DIM), F32)],
        scratch_shapes=[pltpu.VMEM((tc + SUBLANES, GDN_CONV_DIM), F32),
                        pltpu.VMEM((GDN_HEADS, GDN_DK, GDN_DV), F32)],
        compiler_params=_params(("parallel", "arbitrary")),
        name="gdn",
    )(gx, gates, gz, buf, conv_w, s0, norm_g)


def _even_out_kernel(x_ref, oft_ref, og_ref, w_ref, g_ref, b_ref, o_ref):
    y = lax.dot_general(oft_ref[0], w_ref[0:FOX_W, :], (((0,), (0,)), ((), ())), preferred_element_type=F32)
    y = y + _dot(og_ref[0], w_ref[FOX_W:FOX_W + GDN_VW, :])
    o_ref[0] = _layer_norm(ALPHA * x_ref[0] + y, g_ref[...], b_ref[...])


def _even_out(x, o_fox_t, o_gdn, w, g, b):
    bsz, t, _ = x.shape
    tm = min(FFN_ROWS, t)
    row_spec = lambda width: pl.BlockSpec((1, tm, width), lambda bb, i: (bb, i, 0))
    return pl.pallas_call(
        _even_out_kernel,
        grid=(bsz, t // tm),
        in_specs=[row_spec(D_MODEL),
                  pl.BlockSpec((1, FOX_W, tm), lambda bb, i: (bb, 0, i)),
                  row_spec(GDN_VW),
                  _const_spec((FOX_W + GDN_VW, D_MODEL)), _const_spec((1, D_MODEL)), _const_spec((1, D_MODEL))],
        out_specs=row_spec(D_MODEL),
        out_shape=jax.ShapeDtypeStruct((bsz, t, D_MODEL), F32),
        compiler_params=_params(("parallel", "parallel")),
        name="even_out",
    )(x, o_fox_t, o_gdn, w, g, b)


def _odd_kernel(x_ref, h0r_ref, h0i_ref, win_ref, bre_ref, bim_ref, cre_ref, cim_ref, sw_ref, d_ref,
                gw_ref, gb_ref, wout_ref, g_ref, b_ref,
                o_ref, hr_out, hi_out, sre, sim, hr_scr, hi_scr):
    t = pl.program_id(1)
    tm = x_ref.shape[1]
    nblk = D_MODEL // MXU_DIM
    per = S5_LANES // nblk

    @pl.when(t == 0)
    def _():
        hr_scr[...] = h0r_ref[0]
        hi_scr[...] = h0i_ref[0]

    x = x_ref[0]
    u = _dot(x.astype(BF16), win_ref[...])
    ub = u.astype(BF16)
    for kk in range(nblk):
        uk = ub[:, MXU_DIM * kk:MXU_DIM * (kk + 1)]
        sre[:, per * kk:per * (kk + 1)] = _dot(uk, bre_ref[kk])
        sim[:, per * kk:per * (kk + 1)] = _dot(uk, bim_ref[kk])

    ncol = S5_LANES // S5_COLS

    def slab(r, carry):
        rows = pl.ds(pl.multiple_of(r * SUBLANES, SUBLANES), SUBLANES)
        new = []
        for cb in range(ncol):
            cs = slice(S5_COLS * cb, S5_COLS * (cb + 1))
            hr, hi = carry[cb]
            xr = sre[rows, cs]
            xi = sim[rows, cs]
            for idx in range(3):
                wr = sw_ref[2 * idx, :, cs]
                wi = sw_ref[2 * idx + 1, :, cs]
                sr = pltpu.roll(xr, 1 << idx, 0)
                si = pltpu.roll(xi, 1 << idx, 0)
                xr, xi = xr + (wr * sr - wi * si), xi + (wr * si + wi * sr)
            pr = sw_ref[6, :, cs]
            pi = sw_ref[7, :, cs]
            xr, xi = xr + (pr * hr - pi * hi), xi + (pr * hi + pi * hr)
            sre[rows, cs] = xr
            sim[rows, cs] = xi
            new.append((xr[SUBLANES - 1:SUBLANES], xi[SUBLANES - 1:SUBLANES]))
        return tuple(new)

    init = tuple((hr_scr[:, S5_COLS * cb:S5_COLS * (cb + 1)], hi_scr[:, S5_COLS * cb:S5_COLS * (cb + 1)])
                 for cb in range(ncol))
    fin = lax.fori_loop(0, tm // SUBLANES, slab, init)
    for cb in range(ncol):
        hr_scr[:, S5_COLS * cb:S5_COLS * (cb + 1)] = fin[cb][0]
        hi_scr[:, S5_COLS * cb:S5_COLS * (cb + 1)] = fin[cb][1]
    hr_out[0] = hr_scr[...]
    hi_out[0] = hi_scr[...]

    ys = []
    for kk in range(nblk):
        s_re = sre[:, per * kk:per * (kk + 1)].astype(BF16)
        s_im = sim[:, per * kk:per * (kk + 1)].astype(BF16)
        ys.append(_dot(s_re, cre_ref[kk]) - _dot(s_im, cim_ref[kk]))
    y = jnp.concatenate(ys, axis=1) + d_ref[...] * u
    zz = jax.nn.gelu(y)
    gated = zz * jax.nn.sigmoid(_dot(zz.astype(BF16), gw_ref[...]) + gb_ref[...])
    out = _dot(gated.astype(BF16), wout_ref[...])
    o_ref[0] = _layer_norm(ALPHA * x + out, g_ref[...], b_ref[...])


def _odd(x, h0r, h0i, w_in, bre, bim, cre, cim, sw, d, gw, gb, w_out, g, b):
    bsz, t, _ = x.shape
    tm = min(ODD_ROWS, t)
    nblk = D_MODEL // MXU_DIM
    per = S5_LANES // nblk
    st_spec = pl.BlockSpec((1, 1, S5_LANES), lambda bb, i: (bb, 0, 0))
    return pl.pallas_call(
        _odd_kernel,
        grid=(bsz, t // tm),
        in_specs=[pl.BlockSpec((1, tm, D_MODEL), lambda bb, i: (bb, i, 0)), st_spec, st_spec,
                  _const_spec((D_MODEL, D_MODEL)),
                  _const_spec((nblk, MXU_DIM, per)), _const_spec((nblk, MXU_DIM, per)),
                  _const_spec((nblk, per, MXU_DIM)), _const_spec((nblk, per, MXU_DIM)),
                  _const_spec((8, SUBLANES, S5_LANES)), _const_spec((1, D_MODEL)),
                  _const_spec((D_MODEL, D_MODEL)), _const_spec((1, D_MODEL)), _const_spec((D_MODEL, D_MODEL)),
                  _const_spec((1, D_MODEL)), _const_spec((1, D_MODEL))],
        out_specs=[pl.BlockSpec((1, tm, D_MODEL), lambda bb, i: (bb, i, 0)), st_spec, st_spec],
        out_shape=[jax.ShapeDtypeStruct((bsz, t, D_MODEL), F32),
                   jax.ShapeDtypeStruct((bsz, 1, S5_LANES), F32),
                   jax.ShapeDtypeStruct((bsz, 1, S5_LANES), F32)],
        scratch_shapes=[pltpu.VMEM((tm, S5_LANES), F32), pltpu.VMEM((tm, S5_LANES), F32),
                        pltpu.VMEM((1, S5_LANES), F32), pltpu.VMEM((1, S5_LANES), F32)],
        compiler_params=_params(("parallel", "arbitrary")),
        name="odd_s5",
    )(x, h0r, h0i, w_in, bre, bim, cre, cim, sw, d, gw, gb, w_out, g, b)


def _prep_even(w_in, b_f, a_log, dt_bias):
    offs = [0]
    for s in EVEN_SPLITS:
        offs.append(offs[-1] + s)
    fq, fk, fv, ff, gx, ga, gb, gz = [w_in[:, offs[i]:offs[i + 1]] for i in range(len(EVEN_SPLITS))]

    def widen(w):
        w = w.reshape(D_MODEL, FOX_HEADS, FOX_DH)
        return jnp.pad(w, ((0, 0), (0, 0), (0, LANES - FOX_DH))).reshape(D_MODEL, FOX_XW)

    pad = jnp.zeros((D_MODEL, LANES - (2 * FOX_HEADS + 2 * GDN_HEADS)), w_in.dtype)
    w = jnp.concatenate([widen(fq), widen(fk), fk, fv, gx, gz, ff, ga, gb, ff, pad], axis=1).astype(BF16)
    zpad = jnp.zeros((LANES - (2 * FOX_HEADS + 2 * GDN_HEADS),), F32)
    bias = jnp.concatenate([b_f.astype(F32), dt_bias.astype(F32), jnp.zeros((GDN_HEADS,), F32), b_f.astype(F32), zpad])
    alog = jnp.concatenate([jnp.zeros((FOX_HEADS,), F32), a_log.astype(F32),
                            jnp.zeros((LANES - FOX_HEADS - GDN_HEADS,), F32)])
    return w, bias.reshape(1, LANES), alog.reshape(1, LANES)


def _block_diag(blocks):
    nblk, g, r, c = blocks.shape
    eye = jnp.eye(g, dtype=blocks.dtype)
    return jnp.einsum('kgrc,gh->kgrhc', blocks, eye).reshape(nblk, g * r, g * c)


def _prep_odd(lam_re, lam_im, log_step, b_re, b_im, c_re, c_im):
    lam = lax.complex(lam_re.astype(F32), lam_im.astype(F32))
    lam_bar = jnp.exp(lam * jnp.exp(log_step.astype(F32))[:, None])
    b_bar = ((lam_bar - 1.0) / lam)[..., None] * lax.complex(b_re.astype(F32), b_im.astype(F32))
    nblk = D_MODEL // MXU_DIM
    gper = S5_GROUPS // nblk
    bt = jnp.swapaxes(b_bar, 1, 2).reshape(nblk, gper, S5_GROUP, S5_STATE)
    bre = _block_diag(jnp.real(bt)).astype(BF16)
    bim = _block_diag(jnp.imag(bt)).astype(BF16)
    ct_re = jnp.swapaxes(c_re.astype(F32), 1, 2).reshape(nblk, gper, S5_STATE, S5_GROUP)
    ct_im = jnp.swapaxes(c_im.astype(F32), 1, 2).reshape(nblk, gper, S5_STATE, S5_GROUP)
    cre = _block_diag(ct_re).astype(BF16)
    cim = _block_diag(ct_im).astype(BF16)
    lam1 = lam_bar.reshape(1, S5_LANES)
    lam2 = lam1 * lam1
    lam4 = lam2 * lam2
    rows = jnp.arange(SUBLANES)[:, None]
    pw = [lam1]
    for _ in range(SUBLANES - 1):
        pw.append(pw[-1] * lam1)
    pw = jnp.concatenate(pw, axis=0)
    parts = []
    for sh, lm in ((1, lam1), (2, lam2), (4, lam4)):
        wgt = jnp.where(rows >= sh, jnp.broadcast_to(lm, (SUBLANES, S5_LANES)), 0.0)
        parts += [jnp.real(wgt), jnp.imag(wgt)]
    parts += [jnp.real(pw), jnp.imag(pw)]
    return bre, bim, cre, cim, jnp.stack(parts).astype(F32)


def _even_layer(x, fox_past, s0, buf0, w_in, bias, alog, conv_w, norm_g, w_out, g, b):
    bsz, t, _ = x.shape
    q, kx, vt, k, v, gx, gz, gates = _even_in(x, w_in, bias, alog)
    if fox_past is None:
        o_fox_t = _fox_prompt(q, kx, vt)
    else:
        o_fox_t = _fox_sample(q, k, v, gates, *fox_past)
    o_gdn, s_new, new_buf = _gdn(gx, gates, gz, buf0, conv_w, s0, norm_g)
    y = _even_out(x, o_fox_t, o_gdn, w_out, g, b)
    state = (k.reshape(bsz, t, FOX_HEADS, FOX_DH), v.reshape(bsz, t, FOX_HEADS, FOX_DH),
             gates[:, :, :FOX_HEADS], s_new, new_buf)
    return y, state


def kernel(x_prompt, x_sample, cache_fox_k, cache_fox_v, cache_fox_logf, state_gdn, state_gdn_conv, state_s5_re, state_s5_im, ffn_w_in, ffn_w_out, ln_g, ln_b, even_w_in, fox_b_f, gdn_conv_w, gdn_a_log, gdn_dt_bias, gdn_norm_g, even_w_out, odd_w_in, s5_lam_re, s5_lam_im, s5_log_step, s5_b_re, s5_b_im, s5_c_re, s5_c_im, s5_d, s5_glu_w, s5_glu_b, odd_w_out):
    bp, tp, _ = x_prompt.shape
    bs, ts, _ = x_sample.shape
    past = cache_fox_k.shape[2]
    xp, xs = x_prompt, x_sample
    even_p, even_s, odd_p, odd_s = [], [], [], []
    row = lambda a: a.astype(F32).reshape(1, -1)

    def ffn_pair(xp, xs, layer, idx, ln_idx):
        w_in = ffn_w_in[layer, idx].astype(BF16)
        w_out = ffn_w_out[layer, idx].astype(BF16)
        g, b = row(ln_g[layer, ln_idx]), row(ln_b[layer, ln_idx])
        xp = _ffn(xp.reshape(bp * tp, D_MODEL), w_in, w_out, g, b).reshape(bp, tp, D_MODEL)
        xs = _ffn(xs.reshape(bs * ts, D_MODEL), w_in, w_out, g, b).reshape(bs, ts, D_MODEL)
        return xp, xs

    for layer in range(DEPTH):
        xp, xs = ffn_pair(xp, xs, layer, 0, 0)
        j = layer // 2
        g1, b1 = row(ln_g[layer, 1]), row(ln_b[layer, 1])
        if layer % 2 == 0:
            w_in, bias, alog = _prep_even(even_w_in[j], fox_b_f[j], gdn_a_log[j], gdn_dt_bias[j])
            common = (w_in, bias, alog, gdn_conv_w[j].astype(F32), row(gdn_norm_g[j]), even_w_out[j].astype(BF16), g1, b1)
            s0 = jnp.zeros((bp, GDN_HEADS, GDN_DK, GDN_DV), F32)
            buf0 = jnp.zeros((bp, CONV_W - 1, GDN_CONV_DIM), F32)
            xp, st_p = _even_layer(xp, None, s0, buf0, *common)
            lc = jnp.pad(cache_fox_logf[j].astype(F32), ((0, 0), (0, 0), (0, LANES - FOX_HEADS)))
            fox_past = (cache_fox_k[j].reshape(bs, past, FOX_W), cache_fox_v[j].reshape(bs, past, FOX_W), lc)
            xs, st_s = _even_layer(xs, fox_past, state_gdn[j].astype(F32), state_gdn_conv[j].astype(F32), *common)
            even_p.append(st_p)
            even_s.append(st_s)
        else:
            bre, bim, cre, cim, sw = _prep_odd(s5_lam_re[j], s5_lam_im[j], s5_log_step[j], s5_b_re[j], s5_b_im[j],
                                               s5_c_re[j], s5_c_im[j])
            common = (odd_w_in[j].astype(BF16), bre, bim, cre, cim, sw, row(s5_d[j]), s5_glu_w[j].astype(BF16),
                      row(s5_glu_b[j]), odd_w_out[j].astype(BF16), g1, b1)
            h0 = jnp.zeros((bp, 1, S5_LANES), F32)
            xp, hr, hi = _odd(xp, h0, h0, *common)
            odd_p.append((hr.reshape(bp, S5_GROUPS, S5_STATE), hi.reshape(bp, S5_GROUPS, S5_STATE)))
            xs, hr, hi = _odd(xs, state_s5_re[j].astype(F32).reshape(bs, 1, S5_LANES),
                              state_s5_im[j].astype(F32).reshape(bs, 1, S5_LANES), *common)
            odd_s.append((hr.reshape(bs, S5_GROUPS, S5_STATE), hi.reshape(bs, S5_GROUPS, S5_STATE)))
        xp, xs = ffn_pair(xp, xs, layer, 1, 2)

    stack = lambda states, idx: jnp.stack([s[idx] for s in states])
    return (xp, xs,
            stack(even_p, 0), stack(even_p, 1), stack(even_p, 2), stack(even_p, 3), stack(even_p, 4),
            stack(odd_p, 0), stack(odd_p, 1),
            stack(even_s, 0), stack(even_s, 1), stack(even_s, 2), stack(even_s, 3), stack(even_s, 4),
            stack(odd_s, 0), stack(odd_s, 1))
```

```python
import functools

import jax
import jax.numpy as jnp
from jax import lax
from jax.experimental import pallas as pl
from jax.experimental.pallas import tpu as pltpu

F32 = jnp.float32
BF16 = jnp.bfloat16

D_MODEL = 1024
DEPTH = 2
CHUNK = 64
FOX_HEADS = 8
FOX_DH = 64
FOX_W = FOX_HEADS * FOX_DH
GDN_HEADS = 4
GDN_DK = 128
GDN_DV = 128
GDN_QK = GDN_HEADS * GDN_DK
GDN_VW = GDN_HEADS * GDN_DV
GDN_CONV_DIM = 2 * GDN_QK + GDN_VW
CONV_W = 4
S5_GROUP = 16
S5_GROUPS = D_MODEL // S5_GROUP
S5_STATE = 64
S5_LANES = S5_GROUPS * S5_STATE
D_FF = ((8 * D_MODEL // 3 + 127) // 128) * 128
ALPHA = (2.0 * DEPTH) ** 0.25
LN_EPS = 1e-5
NORM_EPS = 1e-6
EVEN_SPLITS = (FOX_W, FOX_W, FOX_W, FOX_HEADS, GDN_CONV_DIM, GDN_HEADS, GDN_HEADS, GDN_VW)

LANES = 128
SUBLANES = 8
MXU_DIM = 256
VMEM_LIMIT = 56 * 1024 * 1024

FOX_XW = FOX_HEADS * LANES
_Q0 = 0
_KX0 = _Q0 + FOX_XW
_K0 = _KX0 + FOX_XW
_V0 = _K0 + FOX_W
_X0 = _V0 + FOX_W
_Z0 = _X0 + GDN_CONV_DIM
_G0 = _Z0 + GDN_VW
EVEN_COLS = _G0 + LANES
_BIAS_LANE = FOX_DH
_ONES_ROW = FOX_DH
_LOGF_LANE, _G_LANE, _BETA_LANE, _CUM_LANE = 0, FOX_HEADS, FOX_HEADS + GDN_HEADS, FOX_HEADS + 2 * GDN_HEADS
NEG_BIG = -1e30

FFN_ROWS = 512
FFN_COLS = 256
EVEN_ROWS = 512
ATT_ROWS = 512
ATT_HEADS = 2
GDN_ROWS = 256
ODD_ROWS = 256
SCAN_UNROLL = 8


def _dot(a, b):
    return jnp.dot(a, b, preferred_element_type=F32)


def _dot_nt(a, b):
    return lax.dot_general(a, b, (((1,), (1,)), ((), ())), preferred_element_type=F32)


def _layer_norm(z, g, b):
    mu = jnp.mean(z, -1, keepdims=True)
    d = z - mu
    var = jnp.mean(d * d, -1, keepdims=True)
    return d * lax.rsqrt(var + LN_EPS) * g + b


def _split3(x):
    hi = x.astype(BF16)
    r = x - hi.astype(F32)
    mid = r.astype(BF16)
    lo = (r - mid.astype(F32)).astype(BF16)
    return hi, mid, lo


def _tri_cumsum(tri, x):
    hi, mid, lo = _split3(x)
    return _dot(tri, hi) + _dot(tri, mid) + _dot(tri, lo)


def _const_spec(shape):
    nd = len(shape)
    return pl.BlockSpec(shape, lambda *_: (0,) * nd, pipeline_mode=pl.Buffered(1))


def _params(sem):
    return pltpu.CompilerParams(dimension_semantics=sem, vmem_limit_bytes=VMEM_LIMIT)


def _ffn_kernel(x_ref, win_ref, wout_ref, g_ref, b_ref, o_ref, h_ref):
    x = x_ref[...]
    xb = x.astype(BF16)
    for c in range(D_FF // FFN_COLS):
        lo = c * FFN_COLS
        a = _dot(xb, win_ref[:, lo:lo + FFN_COLS])
        b = _dot(xb, win_ref[:, D_FF + lo:D_FF + lo + FFN_COLS])
        h_ref[:, lo:lo + FFN_COLS] = (jax.nn.silu(a) * b).astype(BF16)
    y = _dot(h_ref[...], wout_ref[...])
    o_ref[...] = _layer_norm(ALPHA * x + 0.5 * y, g_ref[...], b_ref[...])


def _ffn(x, w_in, w_out, g, b):
    n = x.shape[0]
    tm = min(FFN_ROWS, n)
    return pl.pallas_call(
        _ffn_kernel,
        grid=(n // tm,),
        in_specs=[pl.BlockSpec((tm, D_MODEL), lambda i: (i, 0)),
                  _const_spec((D_MODEL, 2 * D_FF)),
                  _const_spec((D_FF, D_MODEL)),
                  _const_spec((1, D_MODEL)),
                  _const_spec((1, D_MODEL))],
        out_specs=pl.BlockSpec((tm, D_MODEL), lambda i: (i, 0)),
        out_shape=jax.ShapeDtypeStruct((n, D_MODEL), F32),
        scratch_shapes=[pltpu.VMEM((tm, D_FF), BF16)],
        compiler_params=_params(("parallel",)),
        name="ffn_ln",
    )(x, w_in, w_out, g, b)


def _even_in_kernel(x_ref, w_ref, bias_ref, alog_ref,
                    q_ref, kx_ref, vt_ref, k_ref, v_ref, gx_ref, gz_ref, gates_ref,
                    carry_ref):
    t = pl.program_id(1)
    tm = x_ref.shape[1]

    @pl.when(t == 0)
    def _():
        carry_ref[...] = jnp.zeros_like(carry_ref)

    xb = x_ref[0].astype(BF16)

    def mm(lo, width):
        return _dot(xb, w_ref[:, lo:lo + width])

    lane = lax.broadcasted_iota(jnp.int32, (tm, LANES), 1)
    raw = mm(_G0, LANES) + bias_ref[...]
    logf = jax.nn.log_sigmoid(raw)
    gdec = -jnp.exp(alog_ref[...]) * jax.nn.softplus(raw)
    beta = jax.nn.sigmoid(raw)
    gates = jnp.where(lane < _G_LANE, logf, jnp.where(lane < _BETA_LANE, gdec, jnp.where(lane < _CUM_LANE, beta, logf)))
    row = lax.broadcasted_iota(jnp.int32, (tm, tm), 0)
    col = lax.broadcasted_iota(jnp.int32, (tm, tm), 1)
    tri = jnp.where(row >= col, 1.0, 0.0).astype(BF16)
    cum = _tri_cumsum(tri, logf) + carry_ref[...]
    carry_ref[...] = cum[tm - 1:tm, :]
    gates_ref[0] = jnp.where((lane >= _CUM_LANE) & (lane < _CUM_LANE + FOX_HEADS), cum, gates)

    k = mm(_K0, FOX_W)
    k_ref[0] = k
    v = mm(_V0, FOX_W)
    v_ref[0] = v
    gx_ref[0] = mm(_X0, GDN_CONV_DIM)
    gz_ref[0] = mm(_Z0, GDN_VW)

    c_parts = [part.astype(F32) for part in _split3(cum)]
    qx = mm(_Q0, FOX_XW) * (FOX_DH ** -0.5)
    kx = mm(_KX0, FOX_XW)
    for h in range(FOX_HEADS):
        hs = slice(LANES * h, LANES * (h + 1))
        ch = [part[:, _CUM_LANE + h:_CUM_LANE + h + 1] for part in c_parts]
        q_ext = jnp.where(lane < _BIAS_LANE + 3, 0.0, jnp.where(lane < _BIAS_LANE + 6, 1.0, 0.0))
        k_ext = jnp.where(lane < _BIAS_LANE + 3, 1.0, 0.0)
        for i in range(3):
            q_ext = jnp.where(lane == _BIAS_LANE + i, ch[i], q_ext)
            k_ext = jnp.where(lane == _BIAS_LANE + 3 + i, -ch[i], k_ext)
        q_ref[0, :, hs] = jnp.where(lane < FOX_DH, qx[:, hs], q_ext).astype(BF16)
        kx_ref[0, :, hs] = jnp.where(lane < FOX_DH, kx[:, hs], k_ext).astype(BF16)

    v_t = v.T
    pack = 2 * SUBLANES
    ones_blk = jnp.where(lax.broadcasted_iota(jnp.int32, (pack, tm), 0) == 0, 1.0, 0.0).astype(BF16)
    for h in range(FOX_HEADS):
        base = LANES * h
        vt_ref[0, base:base + FOX_DH, :] = v_t[FOX_DH * h:FOX_DH * (h + 1), :].astype(BF16)
        vt_ref[0, base + _ONES_ROW:base + _ONES_ROW + pack, :] = ones_blk
        vt_ref[0, base + _ONES_ROW + pack:base + LANES, :] = jnp.zeros((LANES - _ONES_ROW - pack, tm), BF16)


def _even_in(x, w, bias, alog):
    bsz, t, _ = x.shape
    tm = min(EVEN_ROWS, t)
    row_spec = lambda width: pl.BlockSpec((1, tm, width), lambda b, i: (b, i, 0))
    shp = lambda width, dt: jax.ShapeDtypeStruct((bsz, t, width), dt)
    return pl.pallas_call(
        _even_in_kernel,
        grid=(bsz, t // tm),
        in_specs=[row_spec(D_MODEL), _const_spec((D_MODEL, EVEN_COLS)), _const_spec((1, LANES)), _const_spec((1, LANES))],
        out_specs=[row_spec(FOX_XW), row_spec(FOX_XW),
                   pl.BlockSpec((1, FOX_XW, tm), lambda b, i: (b, 0, i)),
                   row_spec(FOX_W), row_spec(FOX_W), row_spec(GDN_CONV_DIM), row_spec(GDN_VW), row_spec(LANES)],
        out_shape=[shp(FOX_XW, BF16), shp(FOX_XW, BF16), jax.ShapeDtypeStruct((bsz, FOX_XW, t), BF16),
                   shp(FOX_W, F32), shp(FOX_W, F32), shp(GDN_CONV_DIM, F32), shp(GDN_VW, F32), shp(LANES, F32)],
        scratch_shapes=[pltpu.VMEM((1, LANES), F32)],
        compiler_params=_params(("parallel", "arbitrary")),
        name="even_in",
    )(x, w, bias, alog)


def _fox_prompt_kernel(q_ref, k_ref, vt_ref, o_ref, m_ref, acc_ref, s0_ref, s1_ref):
    qi = pl.program_id(2)
    tq = q_ref.shape[1]
    nh = q_ref.shape[2] // LANES
    heads = [slice(LANES * a, LANES * (a + 1)) for a in range(nh)]
    m_ref[...] = jnp.full(m_ref.shape, NEG_BIG, F32)
    acc_ref[...] = jnp.zeros_like(acc_ref)
    key_pos = lax.broadcasted_iota(jnp.int32, (tq, tq), 0)
    qry_pos = lax.broadcasted_iota(jnp.int32, (tq, tq), 1)

    def scores(j, dst):
        off = pl.multiple_of(j * tq, tq)
        for a in range(nh):
            dst[a] = _dot_nt(k_ref[0, pl.ds(off, tq), heads[a]], q_ref[0, :, heads[a]])

    def consume(j, src, masked):
        off = pl.multiple_of(j * tq, tq)
        for a in range(nh):
            s_t = src[a]
            if masked:
                s_t = jnp.where(key_pos <= qry_pos, s_t, NEG_BIG)
            m_prev = m_ref[a]
            m_new = jnp.maximum(m_prev, jnp.max(s_t, axis=0, keepdims=True))
            p_t = jnp.exp(s_t - m_new).astype(BF16)
            acc_ref[a] = jnp.exp(m_prev - m_new) * acc_ref[a] + _dot(vt_ref[0, heads[a], pl.ds(off, tq)], p_t)
            m_ref[a] = m_new

    scores(0, s0_ref)

    def two_tiles(i, c):
        j = 2 * i
        scores(j + 1, s1_ref)
        consume(j, s0_ref, False)
        scores(j + 2, s0_ref)
        consume(j + 1, s1_ref, False)
        return c

    lax.fori_loop(0, qi // 2, two_tiles, 0)

    @pl.when(qi % 2 == 0)
    def _():
        consume(qi, s0_ref, True)

    @pl.when(qi % 2 == 1)
    def _():
        scores(qi, s1_ref)
        consume(qi - 1, s0_ref, False)
        consume(qi, s1_ref, True)

    for a in range(nh):
        acc = acc_ref[a]
        o_ref[0, FOX_DH * a:FOX_DH * (a + 1), :] = (acc[0:FOX_DH] / acc[_ONES_ROW:_ONES_ROW + 1]).astype(BF16)


def _fox_prompt(q, kx, vt):
    bsz, t, _ = q.shape
    tq = min(ATT_ROWS, t)
    nh = ATT_HEADS
    return pl.pallas_call(
        _fox_prompt_kernel,
        grid=(bsz, FOX_HEADS // nh, t // tq),
        in_specs=[pl.BlockSpec((1, tq, nh * LANES), lambda b, p, i: (b, i, p)),
                  pl.BlockSpec((1, t, nh * LANES), lambda b, p, i: (b, 0, p)),
                  pl.BlockSpec((1, nh * LANES, t), lambda b, p, i: (b, p, 0))],
        out_specs=pl.BlockSpec((1, nh * FOX_DH, tq), lambda b, p, i: (b, p, i)),
        out_shape=jax.ShapeDtypeStruct((bsz, FOX_W, t), BF16),
        scratch_shapes=[pltpu.VMEM((nh, 1, tq), F32), pltpu.VMEM((nh, LANES, tq), F32),
                        pltpu.VMEM((nh, tq, tq), F32), pltpu.VMEM((nh, tq, tq), F32)],
        compiler_params=_params(("parallel", "parallel", "arbitrary")),
        name="fox_prompt",
    )(q, kx, vt)


def _fox_sample_kernel(q_ref, kn_ref, vn_ref, gates_ref, kc_ref, vc_ref, lc_ref, o_ref):
    past = kc_ref.shape[1]
    t = q_ref.shape[1]
    row = lax.broadcasted_iota(jnp.int32, (past, past), 0)
    col = lax.broadcasted_iota(jnp.int32, (past, past), 1)
    tri = jnp.where(row >= col, 1.0, 0.0).astype(BF16)
    cum = _tri_cumsum(tri, lc_ref[0])
    ck_cache = (cum - cum[past - 1:past, :]).T
    gts = gates_ref[0]
    ck_new = gts.T
    kc = kc_ref[0].astype(BF16)
    vc = vc_ref[0].astype(BF16)
    kn = kn_ref[0].astype(BF16)
    vn = vn_ref[0].astype(BF16)
    causal = lax.broadcasted_iota(jnp.int32, (t, t), 1) <= lax.broadcasted_iota(jnp.int32, (t, t), 0)
    lane = lax.broadcasted_iota(jnp.int32, (t, LANES), 1)
    outs = []
    for h in range(FOX_HEADS):
        ls = slice(LANES * (h // 2), LANES * (h // 2 + 1))
        qh = q_ref[0, :, LANES * h:LANES * (h + 1)].astype(F32)
        if h % 2 == 0:
            qh = jnp.where(lane < FOX_DH, qh, 0.0).astype(BF16)
        else:
            qh = jnp.where(lane >= FOX_DH, pltpu.roll(qh, FOX_DH, 1), 0.0).astype(BF16)
        cq = gts[:, _CUM_LANE + h:_CUM_LANE + h + 1]
        s1 = (_dot_nt(qh, kc[:, ls]) + cq) - ck_cache[h:h + 1, :]
        s2 = (_dot_nt(qh, kn[:, ls]) + cq) - ck_new[_CUM_LANE + h:_CUM_LANE + h + 1, :]
        s2 = jnp.where(causal, s2, NEG_BIG)
        m = jnp.maximum(jnp.max(s1, axis=1, keepdims=True), jnp.max(s2, axis=1, keepdims=True))
        p1 = jnp.exp(s1 - m)
        p2 = jnp.exp(s2 - m)
        l = jnp.sum(p1, axis=1, keepdims=True) + jnp.sum(p2, axis=1, keepdims=True)
        outs.append((_dot(p1.astype(BF16), vc[:, ls]) + _dot(p2.astype(BF16), vn[:, ls])) / l)
    o = jnp.concatenate([jnp.where(lane < FOX_DH, outs[2 * p], outs[2 * p + 1]) for p in range(FOX_HEADS // 2)], axis=1)
    o_ref[0] = o.T.astype(BF16)


def _fox_sample(q, k, v, gates, past_k, past_v, past_logf):
    bsz, t, _ = k.shape
    past = past_k.shape[1]
    blk = lambda rows, width: pl.BlockSpec((1, rows, width), lambda b: (b, 0, 0))
    return pl.pallas_call(
        _fox_sample_kernel,
        grid=(bsz,),
        in_specs=[blk(t, FOX_XW), blk(t, FOX_W), blk(t, FOX_W), blk(t, LANES),
                  blk(past, FOX_W), blk(past, FOX_W), blk(past, LANES)],
        out_specs=blk(FOX_W, t),
        out_shape=jax.ShapeDtypeStruct((bsz, FOX_W, t), BF16),
        compiler_params=_params(("parallel",)),
        name="fox_sample",
    )(q, k, v, gates, past_k, past_v, past_logf)


def _gdn_kernel(x_ref, gates_ref, z_ref, buf_ref, cw_ref, s0_ref, ng_ref,
                o_ref, sfin_ref, cbuf_ref, xbuf, s_scr, *, chunk):
    t = pl.program_id(1)
    tc = x_ref.shape[1]
    pad = SUBLANES

    @pl.when(t == 0)
    def _():
        xbuf[0:pad, :] = jnp.zeros((pad, GDN_CONV_DIM), F32)
        xbuf[pad - (CONV_W - 1):pad, :] = buf_ref[0]
        s_scr[...] = s0_ref[0]

    @pl.when(t > 0)
    def _():
        xbuf[0:pad, :] = xbuf[tc:tc + pad, :]

    xbuf[pad:pad + tc, :] = x_ref[0]
    cw = cw_ref[...]
    conv = cw[0:1] * xbuf[pad - 3:pad - 3 + tc, :]
    for j in range(1, CONV_W):
        conv = conv + cw[j:j + 1] * xbuf[pad - 3 + j:pad - 3 + j + tc, :]
    cbuf_ref[0] = xbuf[pad + tc - (CONV_W - 1):pad + tc, :]
    act = jax.nn.silu(conv)

    gts = gates_ref[0]
    shift = chunk.bit_length() - 1
    row = lax.broadcasted_iota(jnp.int32, (tc, tc), 0)
    col = lax.broadcasted_iota(jnp.int32, (tc, tc), 1)
    tri = jnp.where((row >= col) & ((row >> shift) == (col >> shift)), 1.0, 0.0).astype(BF16)
    gcum = _tri_cumsum(tri, gts)
    gcum_t = gcum.T
    r_c = lax.broadcasted_iota(jnp.int32, (chunk, chunk), 0)
    c_c = lax.broadcasted_iota(jnp.int32, (chunk, chunk), 1)
    lower = r_c >= c_c
    strict = r_c > c_c
    eye = jnp.where(r_c == c_c, 1.0, 0.0)
    ng = ng_ref[...]

    heads = range(GDN_HEADS)
    chunks = range(tc // chunk)
    rows = [slice(c * chunk, (c + 1) * chunk) for c in chunks]
    cols = [slice(GDN_DK * h, GDN_DK * (h + 1)) for h in heads]

    kn, kn_b, kbeta_b, qn_b, qg, rhs, gc = [], [], [], [], [], [], []
    for h in heads:
        q = act[:, cols[h]]
        k = act[:, GDN_QK + GDN_DK * h:GDN_QK + GDN_DK * (h + 1)]
        v = act[:, 2 * GDN_QK + GDN_DV * h:2 * GDN_QK + GDN_DV * (h + 1)]
        qn = q * lax.rsqrt(jnp.sum(q * q, -1, keepdims=True) + NORM_EPS) * (GDN_DK ** -0.5)
        kn.append(k * lax.rsqrt(jnp.sum(k * k, -1, keepdims=True) + NORM_EPS))
        beta = gts[:, _BETA_LANE + h:_BETA_LANE + h + 1]
        gc.append(gcum[:, _G_LANE + h:_G_LANE + h + 1])
        eg = jnp.exp(gc[h])
        kbeta = kn[h] * beta
        rhs.append(jnp.concatenate([v * beta, kbeta * eg], axis=1).astype(BF16))
        qg.append((qn * eg).astype(BF16))
        kn_b.append(kn[h].astype(BF16))
        kbeta_b.append(kbeta.astype(BF16))
        qn_b.append(qn.astype(BF16))

    units = [(c, h) for c in chunks for h in heads]
    decay, attn, inv, power = {}, {}, {}, {}
    for c, h in units:
        gcc = gc[h][rows[c]]
        gcr = gcum_t[_G_LANE + h:_G_LANE + h + 1, rows[c]]
        decay[c, h] = jnp.where(lower, jnp.exp(jnp.where(lower, gcc - gcr, 0.0)), 0.0)
    for c, h in units:
        power[c, h] = jnp.where(strict, _dot_nt(kbeta_b[h][rows[c]], kn_b[h][rows[c]]) * decay[c, h], 0.0)
        inv[c, h] = eye - power[c, h]
    p = 1
    while 2 * p < chunk:
        for u in units:
            pb = power[u].astype(BF16)
            power[u] = _dot(pb, pb)
        for u in units:
            inv[u] = inv[u] + _dot(inv[u].astype(BF16), power[u].astype(BF16))
        p *= 2
    uw = {}
    for c, h in units:
        uw[c, h] = _dot(inv[c, h].astype(BF16), rhs[h][rows[c]])
    for c, h in units:
        attn[c, h] = (_dot_nt(qn_b[h][rows[c]], kn_b[h][rows[c]]) * decay[c, h]).astype(BF16)

    for c in chunks:
        state = [s_scr[h] for h in heads]
        g_last = [gc[h][rows[c]][chunk - 1:chunk] for h in heads]
        ws = [_dot(jnp.concatenate([uw[c, h][:, GDN_DV:].astype(BF16), qg[h][rows[c]]], axis=0), state[h].astype(BF16))
              for h in heads]
        v_new = [uw[c, h][:, :GDN_DV] - ws[h][:chunk] for h in heads]
        v_new_b = [v_new[h].astype(BF16) for h in heads]
        kdec_t = [(kn[h][rows[c]] * jnp.exp(g_last[h] - gc[h][rows[c]])).T.astype(BF16) for h in heads]
        for h in heads:
            s_scr[h] = state[h] * jnp.exp(g_last[h]) + _dot(kdec_t[h], v_new_b[h])
        for h in heads:
            o = ws[h][chunk:] + _dot(attn[c, h], v_new_b[h])
            z = z_ref[0, rows[c], cols[h]]
            o = o * lax.rsqrt(jnp.mean(o * o, -1, keepdims=True) + NORM_EPS) * ng * jax.nn.silu(z)
            o_ref[0, rows[c], cols[h]] = o.astype(BF16)

    sfin_ref[0] = s_scr[...]


def _gdn(gx, gates, gz, buf, conv_w, s0, norm_g):
    bsz, t, _ = gx.shape
    chunk = min(t, CHUNK)
    tc = min(GDN_ROWS, t)
    row_spec = lambda width: pl.BlockSpec((1, tc, width), lambda b, i: (b, i, 0))
    return pl.pallas_call(
        functools.partial(_gdn_kernel, chunk=chunk),
        grid=(bsz, t // tc),
        in_specs=[row_spec(GDN_CONV_DIM), row_spec(LANES), row_spec(GDN_VW),
                  pl.BlockSpec((1, CONV_W - 1, GDN_CONV_DIM), lambda b, i: (b, 0, 0)),
                  _const_spec((CONV_W, GDN_CONV_DIM)),
                  pl.BlockSpec((1, GDN_HEADS, GDN_DK, GDN_DV), lambda b, i: (b, 0, 0, 0)),
                  _const_spec((1, GDN_DV))],
        out_specs=[row_spec(GDN_VW),
                   pl.BlockSpec((1, GDN_HEADS, GDN_DK, GDN_DV), lambda b, i: (b, 0, 0, 0)),
                   pl.BlockSpec((1, CONV_W - 1, GDN_CONV_DIM), lambda b, i: (b, 0, 0))],
        out_shape=[jax.ShapeDtypeStruct((bsz, t, GDN_VW), BF16),
                   jax.ShapeDtypeStruct((bsz, GDN_HEADS, GDN_DK, GDN_DV), F32),
                   jax.ShapeDtypeStruct((bsz, CONV_W - 1, GDN_CONV_DIM), F32)],
        scratch_shapes=[pltpu.VMEM((tc + SUBLANES, GDN_CONV_DIM), F32),
                        pltpu.VMEM((GDN_HEADS, GDN_DK, GDN_DV), F32)],
        compiler_params=_params(("parallel", "arbitrary")),
        name="gdn",
    )(gx, gates, gz, buf, conv_w, s0, norm_g)


def _even_out_kernel(x_ref, oft_ref, og_ref, w_ref, g_ref, b_ref, o_ref):
    y = lax.dot_general(oft_ref[0], w_ref[0:FOX_W, :], (((0,), (0,)), ((), ())), preferred_element_type=F32)
    y = y + _dot(og_ref[0], w_ref[FOX_W:FOX_W + GDN_VW, :])
    o_ref[0] = _layer_norm(ALPHA * x_ref[0] + y, g_ref[...], b_ref[...])


def _even_out(x, o_fox_t, o_gdn, w, g, b):
    bsz, t, _ = x.shape
    tm = min(FFN_ROWS, t)
    row_spec = lambda width: pl.BlockSpec((1, tm, width), lambda bb, i: (bb, i, 0))
    return pl.pallas_call(
        _even_out_kernel,
        grid=(bsz, t // tm),
        in_specs=[row_spec(D_MODEL),
                  pl.BlockSpec((1, FOX_W, tm), lambda bb, i: (bb, 0, i)),
                  row_spec(GDN_VW),
                  _const_spec((FOX_W + GDN_VW, D_MODEL)), _const_spec((1, D_MODEL)), _const_spec((1, D_MODEL))],
        out_specs=row_spec(D_MODEL),
        out_shape=jax.ShapeDtypeStruct((bsz, t, D_MODEL), F32),
        compiler_params=_params(("parallel", "parallel")),
        name="even_out",
    )(x, o_fox_t, o_gdn, w, g, b)


def _odd_kernel(x_ref, h0r_ref, h0i_ref, win_ref, bre_ref, bim_ref, cre_ref, cim_ref, lam_ref, d_ref,
                gw_ref, gb_ref, wout_ref, g_ref, b_ref,
                o_ref, hr_out, hi_out, sre, sim, tre, tim, hr_scr, hi_scr):
    t = pl.program_id(1)
    tm = x_ref.shape[1]
    nblk = D_MODEL // MXU_DIM
    per = S5_LANES // nblk

    @pl.when(t == 0)
    def _():
        hr_scr[...] = h0r_ref[0]
        hi_scr[...] = h0i_ref[0]

    x = x_ref[0]
    u = _dot(x.astype(BF16), win_ref[...])
    ub = u.astype(BF16)
    for kk in range(nblk):
        uk = ub[:, MXU_DIM * kk:MXU_DIM * (kk + 1)]
        tre[:, :, per * kk:per * (kk + 1)] = _dot(uk, bre_ref[kk]).reshape(tm, 1, per)
        tim[:, :, per * kk:per * (kk + 1)] = _dot(uk, bim_ref[kk]).reshape(tm, 1, per)

    lr = lam_ref[0]
    li = lam_ref[1]

    def step(i, carry):
        hr, hi = carry
        nr = (lr * hr - li * hi) + tre[i]
        ni = (lr * hi + li * hr) + tim[i]
        tre[i] = nr
        tim[i] = ni
        return nr, ni

    hr, hi = lax.fori_loop(0, tm, step, (hr_scr[...], hi_scr[...]), unroll=SCAN_UNROLL)
    hr_scr[...] = hr
    hi_scr[...] = hi
    hr_out[0] = hr
    hi_out[0] = hi

    sre[...] = tre[...].reshape(tm, S5_LANES)
    sim[...] = tim[...].reshape(tm, S5_LANES)
    ys = []
    for kk in range(nblk):
        s_re = sre[:, per * kk:per * (kk + 1)].astype(BF16)
        s_im = sim[:, per * kk:per * (kk + 1)].astype(BF16)
        ys.append(_dot(s_re, cre_ref[kk]) - _dot(s_im, cim_ref[kk]))
    y = jnp.concatenate(ys, axis=1) + d_ref[...] * u
    zz = jax.nn.gelu(y)
    gated = zz * jax.nn.sigmoid(_dot(zz.astype(BF16), gw_ref[...]) + gb_ref[...])
    out = _dot(gated.astype(BF16), wout_ref[...])
    o_ref[0] = _layer_norm(ALPHA * x + out, g_ref[...], b_ref[...])


def _odd(x, h0r, h0i, w_in, bre, bim, cre, cim, sw, d, gw, gb, w_out, g, b):
    bsz, t, _ = x.shape
    tm = min(ODD_ROWS, t)
    nblk = D_MODEL // MXU_DIM
    per = S5_LANES // nblk
    st_spec = pl.BlockSpec((1, 1, S5_LANES), lambda bb, i: (bb, 0, 0))
    return pl.pallas_call(
        _odd_kernel,
        grid=(bsz, t // tm),
        in_specs=[pl.BlockSpec((1, tm, D_MODEL), lambda bb, i: (bb, i, 0)), st_spec, st_spec,
                  _const_spec((D_MODEL, D_MODEL)),
                  _const_spec((nblk, MXU_DIM, per)), _const_spec((nblk, MXU_DIM, per)),
                  _const_spec((nblk, per, MXU_DIM)), _const_spec((nblk, per, MXU_DIM)),
                  _const_spec((2, 1, S5_LANES)), _const_spec((1, D_MODEL)),
                  _const_spec((D_MODEL, D_MODEL)), _const_spec((1, D_MODEL)), _const_spec((D_MODEL, D_MODEL)),
                  _const_spec((1, D_MODEL)), _const_spec((1, D_MODEL))],
        out_specs=[pl.BlockSpec((1, tm, D_MODEL), lambda bb, i: (bb, i, 0)), st_spec, st_spec],
        out_shape=[jax.ShapeDtypeStruct((bsz, t, D_MODEL), F32),
                   jax.ShapeDtypeStruct((bsz, 1, S5_LANES), F32),
                   jax.ShapeDtypeStruct((bsz, 1, S5_LANES), F32)],
        scratch_shapes=[pltpu.VMEM((tm, S5_LANES), F32), pltpu.VMEM((tm, S5_LANES), F32),
                        pltpu.VMEM((tm, 1, S5_LANES), F32), pltpu.VMEM((tm, 1, S5_LANES), F32),
                        pltpu.VMEM((1, S5_LANES), F32), pltpu.VMEM((1, S5_LANES), F32)],
        compiler_params=_params(("parallel", "arbitrary")),
        name="odd_s5",
    )(x, h0r, h0i, w_in, bre, bim, cre, cim, sw, d, gw, gb, w_out, g, b)


def _prep_even(w_in, b_f, a_log, dt_bias):
    offs = [0]
    for s in EVEN_SPLITS:
        offs.append(offs[-1] + s)
    fq, fk, fv, ff, gx, ga, gb, gz = [w_in[:, offs[i]:offs[i + 1]] for i in range(len(EVEN_SPLITS))]

    def widen(w):
        w = w.reshape(D_MODEL, FOX_HEADS, FOX_DH)
        return jnp.pad(w, ((0, 0), (0, 0), (0, LANES - FOX_DH))).reshape(D_MODEL, FOX_XW)

    pad = jnp.zeros((D_MODEL, LANES - (2 * FOX_HEADS + 2 * GDN_HEADS)), w_in.dtype)
    w = jnp.concatenate([widen(fq), widen(fk), fk, fv, gx, gz, ff, ga, gb, ff, pad], axis=1).astype(BF16)
    zpad = jnp.zeros((LANES - (2 * FOX_HEADS + 2 * GDN_HEADS),), F32)
    bias = jnp.concatenate([b_f.astype(F32), dt_bias.astype(F32), jnp.zeros((GDN_HEADS,), F32), b_f.astype(F32), zpad])
    alog = jnp.concatenate([jnp.zeros((FOX_HEADS,), F32), a_log.astype(F32),
                            jnp.zeros((LANES - FOX_HEADS - GDN_HEADS,), F32)])
    return w, bias.reshape(1, LANES), alog.reshape(1, LANES)


def _block_diag(blocks):
    nblk, g, r, c = blocks.shape
    eye = jnp.eye(g, dtype=blocks.dtype)
    return jnp.einsum('kgrc,gh->kgrhc', blocks, eye).reshape(nblk, g * r, g * c)


def _prep_odd(lam_re, lam_im, log_step, b_re, b_im, c_re, c_im):
    lam = lax.complex(lam_re.astype(F32), lam_im.astype(F32))
    lam_bar = jnp.exp(lam * jnp.exp(log_step.astype(F32))[:, None])
    b_bar = ((lam_bar - 1.0) / lam)[..., None] * lax.complex(b_re.astype(F32), b_im.astype(F32))
    nblk = D_MODEL // MXU_DIM
    gper = S5_GROUPS // nblk
    bt = jnp.swapaxes(b_bar, 1, 2).reshape(nblk, gper, S5_GROUP, S5_STATE)
    bre = _block_diag(jnp.real(bt)).astype(BF16)
    bim = _block_diag(jnp.imag(bt)).astype(BF16)
    ct_re = jnp.swapaxes(c_re.astype(F32), 1, 2).reshape(nblk, gper, S5_STATE, S5_GROUP)
    ct_im = jnp.swapaxes(c_im.astype(F32), 1, 2).reshape(nblk, gper, S5_STATE, S5_GROUP)
    cre = _block_diag(ct_re).astype(BF16)
    cim = _block_diag(ct_im).astype(BF16)
    lam_flat = lam_bar.reshape(1, S5_LANES)
    lam_ri = jnp.stack([jnp.real(lam_flat), jnp.imag(lam_flat)]).astype(F32)
    return bre, bim, cre, cim, lam_ri


def _even_layer(x, fox_past, s0, buf0, w_in, bias, alog, conv_w, norm_g, w_out, g, b):
    bsz, t, _ = x.shape
    q, kx, vt, k, v, gx, gz, gates = _even_in(x, w_in, bias, alog)
    if fox_past is None:
        o_fox_t = _fox_prompt(q, kx, vt)
    else:
        o_fox_t = _fox_sample(q, k, v, gates, *fox_past)
    o_gdn, s_new, new_buf = _gdn(gx, gates, gz, buf0, conv_w, s0, norm_g)
    y = _even_out(x, o_fox_t, o_gdn, w_out, g, b)
    state = (k.reshape(bsz, t, FOX_HEADS, FOX_DH), v.reshape(bsz, t, FOX_HEADS, FOX_DH),
             gates[:, :, :FOX_HEADS], s_new, new_buf)
    return y, state


def kernel(x_prompt, x_sample, cache_fox_k, cache_fox_v, cache_fox_logf, state_gdn, state_gdn_conv, state_s5_re, state_s5_im, ffn_w_in, ffn_w_out, ln_g, ln_b, even_w_in, fox_b_f, gdn_conv_w, gdn_a_log, gdn_dt_bias, gdn_norm_g, even_w_out, odd_w_in, s5_lam_re, s5_lam_im, s5_log_step, s5_b_re, s5_b_im, s5_c_re, s5_c_im, s5_d, s5_glu_w, s5_glu_b, odd_w_out):
    bp, tp, _ = x_prompt.shape
    bs, ts, _ = x_sample.shape
    past = cache_fox_k.shape[2]
    xp, xs = x_prompt, x_sample
    even_p, even_s, odd_p, odd_s = [], [], [], []
    row = lambda a: a.astype(F32).reshape(1, -1)

    def ffn_pair(xp, xs, layer, idx, ln_idx):
        w_in = ffn_w_in[layer, idx].astype(BF16)
        w_out = ffn_w_out[layer, idx].astype(BF16)
        g, b = row(ln_g[layer, ln_idx]), row(ln_b[layer, ln_idx])
        xp = _ffn(xp.reshape(bp * tp, D_MODEL), w_in, w_out, g, b).reshape(bp, tp, D_MODEL)
        xs = _ffn(xs.reshape(bs * ts, D_MODEL), w_in, w_out, g, b).reshape(bs, ts, D_MODEL)
        return xp, xs

    for layer in range(DEPTH):
        xp, xs = ffn_pair(xp, xs, layer, 0, 0)
        j = layer // 2
        g1, b1 = row(ln_g[layer, 1]), row(ln_b[layer, 1])
        if layer % 2 == 0:
            w_in, bias, alog = _prep_even(even_w_in[j], fox_b_f[j], gdn_a_log[j], gdn_dt_bias[j])
            common = (w_in, bias, alog, gdn_conv_w[j].astype(F32), row(gdn_norm_g[j]), even_w_out[j].astype(BF16), g1, b1)
            s0 = jnp.zeros((bp, GDN_HEADS, GDN_DK, GDN_DV), F32)
            buf0 = jnp.zeros((bp, CONV_W - 1, GDN_CONV_DIM), F32)
            xp, st_p = _even_layer(xp, None, s0, buf0, *common)
            lc = jnp.pad(cache_fox_logf[j].astype(F32), ((0, 0), (0, 0), (0, LANES - FOX_HEADS)))
            fox_past = (cache_fox_k[j].reshape(bs, past, FOX_W), cache_fox_v[j].reshape(bs, past, FOX_W), lc)
            xs, st_s = _even_layer(xs, fox_past, state_gdn[j].astype(F32), state_gdn_conv[j].astype(F32), *common)
            even_p.append(st_p)
            even_s.append(st_s)
        else:
            bre, bim, cre, cim, sw = _prep_odd(s5_lam_re[j], s5_lam_im[j], s5_log_step[j], s5_b_re[j], s5_b_im[j],
                                               s5_c_re[j], s5_c_im[j])
            common = (odd_w_in[j].astype(BF16), bre, bim, cre, cim, sw, row(s5_d[j]), s5_glu_w[j].astype(BF16),
                      row(s5_glu_b[j]), odd_w_out[j].astype(BF16), g1, b1)
            h0 = jnp.zeros((bp, 1, S5_LANES), F32)
            xp, hr, hi = _odd(xp, h0, h0, *common)
            odd_p.append((hr.reshape(bp, S5_GROUPS, S5_STATE), hi.reshape(bp, S5_GROUPS, S5_STATE)))
            xs, hr, hi = _odd(xs, state_s5_re[j].astype(F32).reshape(bs, 1, S5_LANES),
                              state_s5_im[j].astype(F32).reshape(bs, 1, S5_LANES), *common)
            odd_s.append((hr.reshape(bs, S5_GROUPS, S5_STATE), hi.reshape(bs, S5_GROUPS, S5_STATE)))
        xp, xs = ffn_pair(xp, xs, layer, 1, 2)

    stack = lambda states, idx: jnp.stack([s[idx] for s in states])
    return (xp, xs,
            stack(even_p, 0), stack(even_p, 1), stack(even_p, 2), stack(even_p, 3), stack(even_p, 4),
            stack(odd_p, 0), stack(odd_p, 1),
            stack(even_s, 0), stack(even_s, 1), stack(even_s, 2), stack(even_s, 3), stack(even_s, 4),
            stack(odd_s, 0), stack(odd_s, 1))
```

```python
import functools

import jax
import jax.numpy as jnp
from jax import lax
from jax.experimental import pallas as pl
from jax.experimental.pallas import tpu as pltpu

F32 = jnp.float32
BF16 = jnp.bfloat16

D_MODEL = 1024
DEPTH = 2
CHUNK = 64
FOX_HEADS = 8
FOX_DH = 64
FOX_W = FOX_HEADS * FOX_DH
GDN_HEADS = 4
GDN_DK = 128
GDN_DV = 128
GDN_QK = GDN_HEADS * GDN_DK
GDN_VW = GDN_HEADS * GDN_DV
GDN_CONV_DIM = 2 * GDN_QK + GDN_VW
CONV_W = 4
S5_GROUP = 16
S5_GROUPS = D_MODEL // S5_GROUP
S5_STATE = 64
S5_LANES = S5_GROUPS * S5_STATE
D_FF = ((8 * D_MODEL // 3 + 127) // 128) * 128
ALPHA = (2.0 * DEPTH) ** 0.25
LN_EPS = 1e-5
NORM_EPS = 1e-6
EVEN_SPLITS = (FOX_W, FOX_W, FOX_W, FOX_HEADS, GDN_CONV_DIM, GDN_HEADS, GDN_HEADS, GDN_VW)

LANES = 128
SUBLANES = 8
MXU_DIM = 256
VMEM_LIMIT = 56 * 1024 * 1024

FOX_XW = FOX_HEADS * LANES
_Q0 = 0
_KX0 = _Q0 + FOX_XW
_K0 = _KX0 + FOX_XW
_V0 = _K0 + FOX_W
_X0 = _V0 + FOX_W
_Z0 = _X0 + GDN_CONV_DIM
_G0 = _Z0 + GDN_VW
EVEN_COLS = _G0 + LANES
_BIAS_LANE = FOX_DH
_ONES_ROW = FOX_DH
_LOGF_LANE, _G_LANE, _BETA_LANE, _CUM_LANE = 0, FOX_HEADS, FOX_HEADS + GDN_HEADS, FOX_HEADS + 2 * GDN_HEADS
NEG_BIG = -1e30
LOG2E = 1.4426950408889634

FFN_ROWS = 512
FFN_COLS = 256
EVEN_ROWS = 512
ATT_ROWS = 1024
ATT_HEADS = 2
GDN_ROWS = 256
ODD_ROWS = 256
SCAN_UNROLL = 8
S5_ROWS = S5_LANES // LANES
S5_PITCH = S5_ROWS + SUBLANES


def _dot(a, b):
    return jnp.dot(a, b, preferred_element_type=F32)


def _dot_nt(a, b):
    return lax.dot_general(a, b, (((1,), (1,)), ((), ())), preferred_element_type=F32)


def _layer_norm(z, g, b):
    mu = jnp.mean(z, -1, keepdims=True)
    d = z - mu
    var = jnp.mean(d * d, -1, keepdims=True)
    return d * lax.rsqrt(var + LN_EPS) * g + b


def _split3(x):
    hi = x.astype(BF16)
    r = x - hi.astype(F32)
    mid = r.astype(BF16)
    lo = (r - mid.astype(F32)).astype(BF16)
    return hi, mid, lo


def _tri_cumsum(tri, x):
    hi, mid, lo = _split3(x)
    return _dot(tri, hi) + _dot(tri, mid) + _dot(tri, lo)


def _const_spec(shape):
    nd = len(shape)
    return pl.BlockSpec(shape, lambda *_: (0,) * nd, pipeline_mode=pl.Buffered(1))


def _params(sem):
    return pltpu.CompilerParams(dimension_semantics=sem, vmem_limit_bytes=VMEM_LIMIT)


def _ffn_kernel(x_ref, win_ref, wout_ref, g_ref, b_ref, o_ref, h_ref):
    x = x_ref[...]
    xb = x.astype(BF16)
    for c in range(D_FF // FFN_COLS):
        lo = c * FFN_COLS
        a = _dot(xb, win_ref[:, lo:lo + FFN_COLS])
        b = _dot(xb, win_ref[:, D_FF + lo:D_FF + lo + FFN_COLS])
        h_ref[:, lo:lo + FFN_COLS] = (jax.nn.silu(a) * b).astype(BF16)
    y = _dot(h_ref[...], wout_ref[...])
    o_ref[...] = _layer_norm(ALPHA * x + 0.5 * y, g_ref[...], b_ref[...])


def _ffn(x, w_in, w_out, g, b):
    n = x.shape[0]
    tm = min(FFN_ROWS, n)
    return pl.pallas_call(
        _ffn_kernel,
        grid=(n // tm,),
        in_specs=[pl.BlockSpec((tm, D_MODEL), lambda i: (i, 0)),
                  _const_spec((D_MODEL, 2 * D_FF)),
                  _const_spec((D_FF, D_MODEL)),
                  _const_spec((1, D_MODEL)),
                  _const_spec((1, D_MODEL))],
        out_specs=pl.BlockSpec((tm, D_MODEL), lambda i: (i, 0)),
        out_shape=jax.ShapeDtypeStruct((n, D_MODEL), F32),
        scratch_shapes=[pltpu.VMEM((tm, D_FF), BF16)],
        compiler_params=_params(("parallel",)),
        name="ffn_ln",
    )(x, w_in, w_out, g, b)


def _even_in_kernel(x_ref, w_ref, bias_ref, alog_ref,
                    q_ref, kx_ref, vt_ref, k_ref, v_ref, gx_ref, gz_ref, gates_ref,
                    carry_ref):
    t = pl.program_id(1)
    tm = x_ref.shape[1]

    @pl.when(t == 0)
    def _():
        carry_ref[...] = jnp.zeros_like(carry_ref)

    xb = x_ref[0].astype(BF16)

    def mm(lo, width):
        return _dot(xb, w_ref[:, lo:lo + width])

    lane = lax.broadcasted_iota(jnp.int32, (tm, LANES), 1)
    raw = mm(_G0, LANES) + bias_ref[...]
    logf = jax.nn.log_sigmoid(raw)
    gdec = -jnp.exp(alog_ref[...]) * jax.nn.softplus(raw)
    beta = jax.nn.sigmoid(raw)
    gates = jnp.where(lane < _G_LANE, logf, jnp.where(lane < _BETA_LANE, gdec, jnp.where(lane < _CUM_LANE, beta, logf)))
    row = lax.broadcasted_iota(jnp.int32, (tm, tm), 0)
    col = lax.broadcasted_iota(jnp.int32, (tm, tm), 1)
    tri = jnp.where(row >= col, 1.0, 0.0).astype(BF16)
    cum = _tri_cumsum(tri, logf) + carry_ref[...]
    carry_ref[...] = cum[tm - 1:tm, :]
    gates_ref[0] = jnp.where((lane >= _CUM_LANE) & (lane < _CUM_LANE + FOX_HEADS), cum, gates)

    k = mm(_K0, FOX_W)
    k_ref[0] = k
    v = mm(_V0, FOX_W)
    v_ref[0] = v
    gx_ref[0] = mm(_X0, GDN_CONV_DIM)
    gz_ref[0] = mm(_Z0, GDN_VW)

    c_parts = [part.astype(F32) for part in _split3(cum * LOG2E)]
    qx = mm(_Q0, FOX_XW) * (FOX_DH ** -0.5 * LOG2E)
    kx = mm(_KX0, FOX_XW)
    for h in range(FOX_HEADS):
        hs = slice(LANES * h, LANES * (h + 1))
        ch = [part[:, _CUM_LANE + h:_CUM_LANE + h + 1] for part in c_parts]
        q_ext = jnp.where(lane < _BIAS_LANE + 3, 0.0, jnp.where(lane < _BIAS_LANE + 6, 1.0, 0.0))
        k_ext = jnp.where(lane < _BIAS_LANE + 3, 1.0, 0.0)
        for i in range(3):
            q_ext = jnp.where(lane == _BIAS_LANE + i, ch[i], q_ext)
            k_ext = jnp.where(lane == _BIAS_LANE + 3 + i, -ch[i], k_ext)
        q_ref[0, :, hs] = jnp.where(lane < FOX_DH, qx[:, hs], q_ext).astype(BF16)
        kx_ref[0, :, hs] = jnp.where(lane < FOX_DH, kx[:, hs], k_ext).astype(BF16)

    v_t = v.T
    pack = 2 * SUBLANES
    ones_blk = jnp.where(lax.broadcasted_iota(jnp.int32, (pack, tm), 0) == 0, 1.0, 0.0).astype(BF16)
    for h in range(FOX_HEADS):
        base = LANES * h
        vt_ref[0, base:base + FOX_DH, :] = v_t[FOX_DH * h:FOX_DH * (h + 1), :].astype(BF16)
        vt_ref[0, base + _ONES_ROW:base + _ONES_ROW + pack, :] = ones_blk
        vt_ref[0, base + _ONES_ROW + pack:base + LANES, :] = jnp.zeros((LANES - _ONES_ROW - pack, tm), BF16)


def _even_in(x, w, bias, alog):
    bsz, t, _ = x.shape
    tm = min(EVEN_ROWS, t)
    row_spec = lambda width: pl.BlockSpec((1, tm, width), lambda b, i: (b, i, 0))
    shp = lambda width, dt: jax.ShapeDtypeStruct((bsz, t, width), dt)
    return pl.pallas_call(
        _even_in_kernel,
        grid=(bsz, t // tm),
        in_specs=[row_spec(D_MODEL), _const_spec((D_MODEL, EVEN_COLS)), _const_spec((1, LANES)), _const_spec((1, LANES))],
        out_specs=[row_spec(FOX_XW), row_spec(FOX_XW),
                   pl.BlockSpec((1, FOX_XW, tm), lambda b, i: (b, 0, i)),
                   row_spec(FOX_W), row_spec(FOX_W), row_spec(GDN_CONV_DIM), row_spec(GDN_VW), row_spec(LANES)],
        out_shape=[shp(FOX_XW, BF16), shp(FOX_XW, BF16), jax.ShapeDtypeStruct((bsz, FOX_XW, t), BF16),
                   shp(FOX_W, F32), shp(FOX_W, F32), shp(GDN_CONV_DIM, F32), shp(GDN_VW, F32), shp(LANES, F32)],
        scratch_shapes=[pltpu.VMEM((1, LANES), F32)],
        compiler_params=_params(("parallel", "arbitrary")),
        name="even_in",
    )(x, w, bias, alog)


def _fox_prompt_kernel(q_ref, k_ref, vt_ref, o_ref, m_ref, acc_ref, s0_ref, s1_ref, p0_ref, p1_ref, x0_ref, x1_ref):
    qi = pl.program_id(2)
    tq = q_ref.shape[1]
    tk = s0_ref.shape[1]
    nh = q_ref.shape[2] // LANES
    heads = [slice(LANES * a, LANES * (a + 1)) for a in range(nh)]
    m_ref[...] = jnp.full(m_ref.shape, NEG_BIG, F32)
    acc_ref[...] = jnp.zeros_like(acc_ref)
    key_pos = lax.broadcasted_iota(jnp.int32, (tk, tq), 0)
    qry_pos = lax.broadcasted_iota(jnp.int32, (tk, tq), 1)

    even = (s0_ref, x0_ref, p0_ref)
    odd = (s1_ref, x1_ref, p1_ref)

    def scores(j, dst, key_shift):
        off = pl.multiple_of(j * tk, tk)
        for a in range(nh):
            s_t = _dot_nt(k_ref[0, pl.ds(off, tk), heads[a]], q_ref[0, :, heads[a]])
            if key_shift is not None:
                s_t = jnp.where(key_pos + key_shift <= qry_pos, s_t, NEG_BIG)
            dst[0][a] = s_t
            dst[1][a] = jnp.max(s_t, axis=0, keepdims=True)

    def weighted_values(j, p_src, a):
        off = pl.multiple_of(j * tk, tk)
        return _dot(vt_ref[0, heads[a], pl.ds(off, tk)], p_src[a])

    def step(j, cur, prev, prefetch, next_shift=None):
        if prefetch:
            scores(j + 1, prev, next_shift)
        j_prev = jnp.maximum(j - 1, 0)
        for a in range(nh):
            m_prev = m_ref[a]
            m_new = jnp.maximum(m_prev, cur[1][a])
            r = weighted_values(j_prev, prev[2], a)
            cur[2][a] = jnp.exp2(cur[0][a] - m_new).astype(BF16)
            acc_ref[a] = jnp.exp2(m_prev - m_new) * (acc_ref[a] + r)
            m_ref[a] = m_new

    p1_ref[...] = jnp.zeros_like(p1_ref)

    @pl.when(qi == 0)
    def _():
        scores(0, even, 0)

    @pl.when(qi > 0)
    def _():
        scores(0, even, None)

        def two_tiles(i, c):
            j = 2 * i
            step(j, even, odd, True)
            step(j + 1, odd, even, True)
            return c

        lax.fori_loop(0, qi - 1, two_tiles, 0)
        step(2 * qi - 2, even, odd, True)
        step(2 * qi - 1, odd, even, True, 0)

    step(2 * qi, even, odd, True, tk)
    step(2 * qi + 1, odd, even, False)

    for a in range(nh):
        acc = acc_ref[a] + weighted_values(2 * qi + 1, p1_ref, a)
        o_ref[0, FOX_DH * a:FOX_DH * (a + 1), :] = (acc[0:FOX_DH] / acc[_ONES_ROW:_ONES_ROW + 1]).astype(BF16)


def _fox_prompt(q, kx, vt):
    bsz, t, _ = q.shape
    tq = min(ATT_ROWS, t)
    nh = ATT_HEADS
    return pl.pallas_call(
        _fox_prompt_kernel,
        grid=(bsz, FOX_HEADS // nh, t // tq),
        in_specs=[pl.BlockSpec((1, tq, nh * LANES), lambda b, p, i: (b, i, p)),
                  pl.BlockSpec((1, t, nh * LANES), lambda b, p, i: (b, 0, p)),
                  pl.BlockSpec((1, nh * LANES, t), lambda b, p, i: (b, p, 0))],
        out_specs=pl.BlockSpec((1, nh * FOX_DH, tq), lambda b, p, i: (b, p, i)),
        out_shape=jax.ShapeDtypeStruct((bsz, FOX_W, t), BF16),
        scratch_shapes=[pltpu.VMEM((nh, 1, tq), F32), pltpu.VMEM((nh, LANES, tq), F32),
                        pltpu.VMEM((nh, tq // 2, tq), F32), pltpu.VMEM((nh, tq // 2, tq), F32),
                        pltpu.VMEM((nh, tq // 2, tq), BF16), pltpu.VMEM((nh, tq // 2, tq), BF16),
                        pltpu.VMEM((nh, 1, tq), F32), pltpu.VMEM((nh, 1, tq), F32)],
        compiler_params=_params(("parallel", "parallel", "arbitrary")),
        name="fox_prompt",
    )(q, kx, vt)


def _fox_sample_kernel(q_ref, kn_ref, vn_ref, gates_ref, kc_ref, vc_ref, lc_ref, o_ref):
    past = kc_ref.shape[1]
    t = q_ref.shape[1]
    row = lax.broadcasted_iota(jnp.int32, (past, past), 0)
    col = lax.broadcasted_iota(jnp.int32, (past, past), 1)
    tri = jnp.where(row >= col, 1.0, 0.0).astype(BF16)
    cum = _tri_cumsum(tri, lc_ref[0])
    ck_cache = (cum - cum[past - 1:past, :]).T
    gts = gates_ref[0]
    ck_new = gts.T
    kc = kc_ref[0].astype(BF16)
    vc = vc_ref[0].astype(BF16)
    kn = kn_ref[0].astype(BF16)
    vn = vn_ref[0].astype(BF16)
    causal = lax.broadcasted_iota(jnp.int32, (t, t), 1) <= lax.broadcasted_iota(jnp.int32, (t, t), 0)
    lane = lax.broadcasted_iota(jnp.int32, (t, LANES), 1)
    outs = []
    for h in range(FOX_HEADS):
        ls = slice(LANES * (h // 2), LANES * (h // 2 + 1))
        qh = q_ref[0, :, LANES * h:LANES * (h + 1)].astype(F32)
        if h % 2 == 0:
            qh = jnp.where(lane < FOX_DH, qh, 0.0).astype(BF16)
        else:
            qh = jnp.where(lane >= FOX_DH, pltpu.roll(qh, FOX_DH, 1), 0.0).astype(BF16)
        cq = gts[:, _CUM_LANE + h:_CUM_LANE + h + 1]
        cq = cq * LOG2E
        s1 = (_dot_nt(qh, kc[:, ls]) + cq) - ck_cache[h:h + 1, :] * LOG2E
        s2 = (_dot_nt(qh, kn[:, ls]) + cq) - ck_new[_CUM_LANE + h:_CUM_LANE + h + 1, :] * LOG2E
        s2 = jnp.where(causal, s2, NEG_BIG)
        m = jnp.maximum(jnp.max(s1, axis=1, keepdims=True), jnp.max(s2, axis=1, keepdims=True))
        p1 = jnp.exp2(s1 - m)
        p2 = jnp.exp2(s2 - m)
        l = jnp.sum(p1, axis=1, keepdims=True) + jnp.sum(p2, axis=1, keepdims=True)
        outs.append((_dot(p1.astype(BF16), vc[:, ls]) + _dot(p2.astype(BF16), vn[:, ls])) / l)
    o = jnp.concatenate([jnp.where(lane < FOX_DH, outs[2 * p], outs[2 * p + 1]) for p in range(FOX_HEADS // 2)], axis=1)
    o_ref[0] = o.T.astype(BF16)


def _fox_sample(q, k, v, gates, past_k, past_v, past_logf):
    bsz, t, _ = k.shape
    past = past_k.shape[1]
    blk = lambda rows, width: pl.BlockSpec((1, rows, width), lambda b: (b, 0, 0))
    return pl.pallas_call(
        _fox_sample_kernel,
        grid=(bsz,),
        in_specs=[blk(t, FOX_XW), blk(t, FOX_W), blk(t, FOX_W), blk(t, LANES),
                  blk(past, FOX_W), blk(past, FOX_W), blk(past, LANES)],
        out_specs=blk(FOX_W, t),
        out_shape=jax.ShapeDtypeStruct((bsz, FOX_W, t), BF16),
        compiler_params=_params(("parallel",)),
        name="fox_sample",
    )(q, k, v, gates, past_k, past_v, past_logf)


def _gdn_kernel(x_ref, gates_ref, z_ref, buf_ref, cw_ref, s0_ref, ng_ref,
                o_ref, sfin_ref, cbuf_ref, xbuf, s_scr, *, chunk):
    t = pl.program_id(1)
    tc = x_ref.shape[1]
    pad = SUBLANES

    @pl.when(t == 0)
    def _():
        xbuf[0:pad, :] = jnp.zeros((pad, GDN_CONV_DIM), F32)
        xbuf[pad - (CONV_W - 1):pad, :] = buf_ref[0]
        s_scr[...] = s0_ref[0]

    @pl.when(t > 0)
    def _():
        xbuf[0:pad, :] = xbuf[tc:tc + pad, :]

    xbuf[pad:pad + tc, :] = x_ref[0]
    cw = cw_ref[...]
    conv = cw[0:1] * xbuf[pad - 3:pad - 3 + tc, :]
    for j in range(1, CONV_W):
        conv = conv + cw[j:j + 1] * xbuf[pad - 3 + j:pad - 3 + j + tc, :]
    cbuf_ref[0] = xbuf[pad + tc - (CONV_W - 1):pad + tc, :]
    act = jax.nn.silu(conv)

    gts = gates_ref[0]
    shift = chunk.bit_length() - 1
    row = lax.broadcasted_iota(jnp.int32, (tc, tc), 0)
    col = lax.broadcasted_iota(jnp.int32, (tc, tc), 1)
    tri = jnp.where((row >= col) & ((row >> shift) == (col >> shift)), 1.0, 0.0).astype(BF16)
    gcum = _tri_cumsum(tri, gts)
    gcum_t = gcum.T
    r_c = lax.broadcasted_iota(jnp.int32, (chunk, chunk), 0)
    c_c = lax.broadcasted_iota(jnp.int32, (chunk, chunk), 1)
    lower = r_c >= c_c
    strict = r_c > c_c
    eye = jnp.where(r_c == c_c, 1.0, 0.0)
    ng = ng_ref[...]

    heads = range(GDN_HEADS)
    chunks = range(tc // chunk)
    rows = [slice(c * chunk, (c + 1) * chunk) for c in chunks]
    cols = [slice(GDN_DK * h, GDN_DK * (h + 1)) for h in heads]

    kn, kn_b, kbeta_b, qn_b, qg, rhs, gc = [], [], [], [], [], [], []
    for h in heads:
        q = act[:, cols[h]]
        k = act[:, GDN_QK + GDN_DK * h:GDN_QK + GDN_DK * (h + 1)]
        v = act[:, 2 * GDN_QK + GDN_DV * h:2 * GDN_QK + GDN_DV * (h + 1)]
        qn = q * lax.rsqrt(jnp.sum(q * q, -1, keepdims=True) + NORM_EPS) * (GDN_DK ** -0.5)
        kn.append(k * lax.rsqrt(jnp.sum(k * k, -1, keepdims=True) + NORM_EPS))
        beta = gts[:, _BETA_LANE + h:_BETA_LANE + h + 1]
        gc.append(gcum[:, _G_LANE + h:_G_LANE + h + 1])
        eg = jnp.exp(gc[h])
        kbeta = kn[h] * beta
        rhs.append(jnp.concatenate([v * beta, kbeta * eg], axis=1).astype(BF16))
        qg.append((qn * eg).astype(BF16))
        kn_b.append(kn[h].astype(BF16))
        kbeta_b.append(kbeta.astype(BF16))
        qn_b.append(qn.astype(BF16))

    units = [(c, h) for c in chunks for h in heads]
    decay, attn, inv, power = {}, {}, {}, {}
    for c, h in units:
        gcc = gc[h][rows[c]]
        gcr = gcum_t[_G_LANE + h:_G_LANE + h + 1, rows[c]]
        decay[c, h] = jnp.where(lower, jnp.exp(jnp.where(lower, gcc - gcr, 0.0)), 0.0)
    for c, h in units:
        power[c, h] = jnp.where(strict, _dot_nt(kbeta_b[h][rows[c]], kn_b[h][rows[c]]) * decay[c, h], 0.0)
        inv[c, h] = eye - power[c, h]
    p = 1
    while 2 * p < chunk:
        for u in units:
            pb = power[u].astype(BF16)
            power[u] = _dot(pb, pb)
        for u in units:
            inv[u] = inv[u] + _dot(inv[u].astype(BF16), power[u].astype(BF16))
        p *= 2
    uw = {}
    for c, h in units:
        uw[c, h] = _dot(inv[c, h].astype(BF16), rhs[h][rows[c]])
    for c, h in units:
        attn[c, h] = (_dot_nt(qn_b[h][rows[c]], kn_b[h][rows[c]]) * decay[c, h]).astype(BF16)

    for c in chunks:
        state = [s_scr[h] for h in heads]
        g_last = [gc[h][rows[c]][chunk - 1:chunk] for h in heads]
        ws = [_dot(jnp.concatenate([uw[c, h][:, GDN_DV:].astype(BF16), qg[h][rows[c]]], axis=0), state[h].astype(BF16))
              for h in heads]
        v_new = [uw[c, h][:, :GDN_DV] - ws[h][:chunk] for h in heads]
        v_new_b = [v_new[h].astype(BF16) for h in heads]
        kdec_t = [(kn[h][rows[c]] * jnp.exp(g_last[h] - gc[h][rows[c]])).T.astype(BF16) for h in heads]
        for h in heads:
            s_scr[h] = state[h] * jnp.exp(g_last[h]) + _dot(kdec_t[h], v_new_b[h])
        for h in heads:
            o = ws[h][chunk:] + _dot(attn[c, h], v_new_b[h])
            z = z_ref[0, rows[c], cols[h]]
            o = o * lax.rsqrt(jnp.mean(o * o, -1, keepdims=True) + NORM_EPS) * ng * jax.nn.silu(z)
            o_ref[0, rows[c], cols[h]] = o.astype(BF16)

    sfin_ref[0] = s_scr[...]


def _gdn(gx, gates, gz, buf, conv_w, s0, norm_g):
    bsz, t, _ = gx.shape
    chunk = min(t, CHUNK)
    tc = min(GDN_ROWS, t)
    row_spec = lambda width: pl.BlockSpec((1, tc, width), lambda b, i: (b, i, 0))
    return pl.pallas_call(
        functools.partial(_gdn_kernel, chunk=chunk),
        grid=(bsz, t // tc),
        in_specs=[row_spec(GDN_CONV_DIM), row_spec(LANES), row_spec(GDN_VW),
                  pl.BlockSpec((1, CONV_W - 1, GDN_CONV_DIM), lambda b, i: (b, 0, 0)),
                  _const_spec((CONV_W, GDN_CONV_DIM)),
                  pl.BlockSpec((1, GDN_HEADS, GDN_DK, GDN_DV), lambda b, i: (b, 0, 0, 0)),
                  _const_spec((1, GDN_DV))],
        out_specs=[row_spec(GDN_VW),
                   pl.BlockSpec((1, GDN_HEADS, GDN_DK, GDN_DV), lambda b, i: (b, 0, 0, 0)),
                   pl.BlockSpec((1, CONV_W - 1, GDN_CONV_DIM), lambda b, i: (b, 0, 0))],
        out_shape=[jax.ShapeDtypeStruct((bsz, t, GDN_VW), BF16),
                   jax.ShapeDtypeStruct((bsz, GDN_HEADS, GDN_DK, GDN_DV), F32),
                   jax.ShapeDtypeStruct((bsz, CONV_W - 1, GDN_CONV_DIM), F32)],
        scratch_shapes=[pltpu.VMEM((tc + SUBLANES, GDN_CONV_DIM), F32),
                        pltpu.VMEM((GDN_HEADS, GDN_DK, GDN_DV), F32)],
        compiler_params=_params(("parallel", "arbitrary")),
        name="gdn",
    )(gx, gates, gz, buf, conv_w, s0, norm_g)


def _even_out_kernel(x_ref, oft_ref, og_ref, w_ref, g_ref, b_ref, o_ref):
    y = lax.dot_general(oft_ref[0], w_ref[0:FOX_W, :], (((0,), (0,)), ((), ())), preferred_element_type=F32)
    y = y + _dot(og_ref[0], w_ref[FOX_W:FOX_W + GDN_VW, :])
    o_ref[0] = _layer_norm(ALPHA * x_ref[0] + y, g_ref[...], b_ref[...])


def _even_out(x, o_fox_t, o_gdn, w, g, b):
    bsz, t, _ = x.shape
    tm = min(FFN_ROWS, t)
    row_spec = lambda width: pl.BlockSpec((1, tm, width), lambda bb, i: (bb, i, 0))
    return pl.pallas_call(
        _even_out_kernel,
        grid=(bsz, t // tm),
        in_specs=[row_spec(D_MODEL),
                  pl.BlockSpec((1, FOX_W, tm), lambda bb, i: (bb, 0, i)),
                  row_spec(GDN_VW),
                  _const_spec((FOX_W + GDN_VW, D_MODEL)), _const_spec((1, D_MODEL)), _const_spec((1, D_MODEL))],
        out_specs=row_spec(D_MODEL),
        out_shape=jax.ShapeDtypeStruct((bsz, t, D_MODEL), F32),
        compiler_params=_params(("parallel", "parallel")),
        name="even_out",
    )(x, o_fox_t, o_gdn, w, g, b)


def _odd_kernel(x_ref, h0r_ref, h0i_ref, win_ref, bre_ref, bim_ref, cre_ref, cim_ref, lam_ref, d_ref,
                gw_ref, gb_ref, wout_ref, g_ref, b_ref,
                o_ref, hr_out, hi_out, tre, tim, hr_scr, hi_scr):
    t = pl.program_id(1)
    tm = x_ref.shape[1]
    nblk = D_MODEL // MXU_DIM
    per = S5_LANES // nblk

    @pl.when(t == 0)
    def _():
        hr_scr[...] = h0r_ref[0]
        hi_scr[...] = h0i_ref[0]

    x = x_ref[0]
    u = _dot(x.astype(BF16), win_ref[...])
    ub = u.astype(BF16)
    jper = per // LANES

    def column(j):
        return pl.ds(j, tm, stride=S5_PITCH)

    for kk in range(nblk):
        uk = ub[:, MXU_DIM * kk:MXU_DIM * (kk + 1)]
        b_re = _dot(uk, bre_ref[kk])
        b_im = _dot(uk, bim_ref[kk])
        for jj in range(jper):
            tre[column(jper * kk + jj), :] = b_re[:, LANES * jj:LANES * (jj + 1)]
            tim[column(jper * kk + jj), :] = b_im[:, LANES * jj:LANES * (jj + 1)]

    lr = lam_ref[0]
    li = lam_ref[1]

    def step(i, carry):
        hr, hi = carry
        rows = pl.ds(pl.multiple_of(i * S5_PITCH, SUBLANES), S5_ROWS)
        nr = (lr * hr - li * hi) + tre[rows, :]
        ni = (lr * hi + li * hr) + tim[rows, :]
        tre[rows, :] = nr
        tim[rows, :] = ni
        return nr, ni

    hr, hi = lax.fori_loop(0, tm, step, (hr_scr[...], hi_scr[...]), unroll=SCAN_UNROLL)
    hr_scr[...] = hr
    hi_scr[...] = hi
    hr_out[0] = hr
    hi_out[0] = hi

    ys = []
    for kk in range(nblk):
        s_re = jnp.concatenate([tre[column(jper * kk + jj), :] for jj in range(jper)], axis=1).astype(BF16)
        s_im = jnp.concatenate([tim[column(jper * kk + jj), :] for jj in range(jper)], axis=1).astype(BF16)
        ys.append(_dot(s_re, cre_ref[kk]) - _dot(s_im, cim_ref[kk]))
    y = jnp.concatenate(ys, axis=1) + d_ref[...] * u
    zz = jax.nn.gelu(y)
    gated = zz * jax.nn.sigmoid(_dot(zz.astype(BF16), gw_ref[...]) + gb_ref[...])
    out = _dot(gated.astype(BF16), wout_ref[...])
    o_ref[0] = _layer_norm(ALPHA * x + out, g_ref[...], b_ref[...])


def _odd(x, h0r, h0i, w_in, bre, bim, cre, cim, sw, d, gw, gb, w_out, g, b):
    bsz, t, _ = x.shape
    tm = min(ODD_ROWS, t)
    nblk = D_MODEL // MXU_DIM
    per = S5_LANES // nblk
    st_spec = pl.BlockSpec((1, S5_ROWS, LANES), lambda bb, i: (bb, 0, 0))
    return pl.pallas_call(
        _odd_kernel,
        grid=(bsz, t // tm),
        in_specs=[pl.BlockSpec((1, tm, D_MODEL), lambda bb, i: (bb, i, 0)), st_spec, st_spec,
                  _const_spec((D_MODEL, D_MODEL)),
                  _const_spec((nblk, MXU_DIM, per)), _const_spec((nblk, MXU_DIM, per)),
                  _const_spec((nblk, per, MXU_DIM)), _const_spec((nblk, per, MXU_DIM)),
                  _const_spec((2, S5_ROWS, LANES)), _const_spec((1, D_MODEL)),
                  _const_spec((D_MODEL, D_MODEL)), _const_spec((1, D_MODEL)), _const_spec((D_MODEL, D_MODEL)),
                  _const_spec((1, D_MODEL)), _const_spec((1, D_MODEL))],
        out_specs=[pl.BlockSpec((1, tm, D_MODEL), lambda bb, i: (bb, i, 0)), st_spec, st_spec],
        out_shape=[jax.ShapeDtypeStruct((bsz, t, D_MODEL), F32),
                   jax.ShapeDtypeStruct((bsz, S5_ROWS, LANES), F32),
                   jax.ShapeDtypeStruct((bsz, S5_ROWS, LANES), F32)],
        scratch_shapes=[pltpu.VMEM((tm * S5_PITCH, LANES), F32), pltpu.VMEM((tm * S5_PITCH, LANES), F32),
                        pltpu.VMEM((S5_ROWS, LANES), F32), pltpu.VMEM((S5_ROWS, LANES), F32)],
        compiler_params=_params(("parallel", "arbitrary")),
        name="odd_s5",
    )(x, h0r, h0i, w_in, bre, bim, cre, cim, sw, d, gw, gb, w_out, g, b)


def _prep_even(w_in, b_f, a_log, dt_bias):
    offs = [0]
    for s in EVEN_SPLITS:
        offs.append(offs[-1] + s)
    fq, fk, fv, ff, gx, ga, gb, gz = [w_in[:, offs[i]:offs[i + 1]] for i in range(len(EVEN_SPLITS))]

    def widen(w):
        w = w.reshape(D_MODEL, FOX_HEADS, FOX_DH)
        return jnp.pad(w, ((0, 0), (0, 0), (0, LANES - FOX_DH))).reshape(D_MODEL, FOX_XW)

    pad = jnp.zeros((D_MODEL, LANES - (2 * FOX_HEADS + 2 * GDN_HEADS)), w_in.dtype)
    w = jnp.concatenate([widen(fq), widen(fk), fk, fv, gx, gz, ff, ga, gb, ff, pad], axis=1).astype(BF16)
    zpad = jnp.zeros((LANES - (2 * FOX_HEADS + 2 * GDN_HEADS),), F32)
    bias = jnp.concatenate([b_f.astype(F32), dt_bias.astype(F32), jnp.zeros((GDN_HEADS,), F32), b_f.astype(F32), zpad])
    alog = jnp.concatenate([jnp.zeros((FOX_HEADS,), F32), a_log.astype(F32),
                            jnp.zeros((LANES - FOX_HEADS - GDN_HEADS,), F32)])
    return w, bias.reshape(1, LANES), alog.reshape(1, LANES)


def _block_diag(blocks):
    nblk, g, r, c = blocks.shape
    eye = jnp.eye(g, dtype=blocks.dtype)
    return jnp.einsum('kgrc,gh->kgrhc', blocks, eye).reshape(nblk, g * r, g * c)


def _prep_odd(lam_re, lam_im, log_step, b_re, b_im, c_re, c_im):
    lam = lax.complex(lam_re.astype(F32), lam_im.astype(F32))
    lam_bar = jnp.exp(lam * jnp.exp(log_step.astype(F32))[:, None])
    b_bar = ((lam_bar - 1.0) / lam)[..., None] * lax.complex(b_re.astype(F32), b_im.astype(F32))
    nblk = D_MODEL // MXU_DIM
    gper = S5_GROUPS // nblk
    bt = jnp.swapaxes(b_bar, 1, 2).reshape(nblk, gper, S5_GROUP, S5_STATE)
    bre = _block_diag(jnp.real(bt)).astype(BF16)
    bim = _block_diag(jnp.imag(bt)).astype(BF16)
    ct_re = jnp.swapaxes(c_re.astype(F32), 1, 2).reshape(nblk, gper, S5_STATE, S5_GROUP)
    ct_im = jnp.swapaxes(c_im.astype(F32), 1, 2).reshape(nblk, gper, S5_STATE, S5_GROUP)
    cre = _block_diag(ct_re).astype(BF16)
    cim = _block_diag(ct_im).astype(BF16)
    lam_flat = lam_bar.reshape(S5_ROWS, LANES)
    lam_ri = jnp.stack([jnp.real(lam_flat), jnp.imag(lam_flat)]).astype(F32)
    return bre, bim, cre, cim, lam_ri


def _even_layer(x, fox_past, s0, buf0, w_in, bias, alog, conv_w, norm_g, w_out, g, b):
    bsz, t, _ = x.shape
    q, kx, vt, k, v, gx, gz, gates = _even_in(x, w_in, bias, alog)
    if fox_past is None:
        o_fox_t = _fox_prompt(q, kx, vt)
    else:
        o_fox_t = _fox_sample(q, k, v, gates, *fox_past)
    o_gdn, s_new, new_buf = _gdn(gx, gates, gz, buf0, conv_w, s0, norm_g)
    y = _even_out(x, o_fox_t, o_gdn, w_out, g, b)
    state = (k.reshape(bsz, t, FOX_HEADS, FOX_DH), v.reshape(bsz, t, FOX_HEADS, FOX_DH),
             gates[:, :, :FOX_HEADS], s_new, new_buf)
    return y, state


def kernel(x_prompt, x_sample, cache_fox_k, cache_fox_v, cache_fox_logf, state_gdn, state_gdn_conv, state_s5_re, state_s5_im, ffn_w_in, ffn_w_out, ln_g, ln_b, even_w_in, fox_b_f, gdn_conv_w, gdn_a_log, gdn_dt_bias, gdn_norm_g, even_w_out, odd_w_in, s5_lam_re, s5_lam_im, s5_log_step, s5_b_re, s5_b_im, s5_c_re, s5_c_im, s5_d, s5_glu_w, s5_glu_b, odd_w_out):
    bp, tp, _ = x_prompt.shape
    bs, ts, _ = x_sample.shape
    past = cache_fox_k.shape[2]
    xp, xs = x_prompt, x_sample
    even_p, even_s, odd_p, odd_s = [], [], [], []
    row = lambda a: a.astype(F32).reshape(1, -1)

    def ffn_pair(xp, xs, layer, idx, ln_idx):
        w_in = ffn_w_in[layer, idx].astype(BF16)
        w_out = ffn_w_out[layer, idx].astype(BF16)
        g, b = row(ln_g[layer, ln_idx]), row(ln_b[layer, ln_idx])
        xp = _ffn(xp.reshape(bp * tp, D_MODEL), w_in, w_out, g, b).reshape(bp, tp, D_MODEL)
        xs = _ffn(xs.reshape(bs * ts, D_MODEL), w_in, w_out, g, b).reshape(bs, ts, D_MODEL)
        return xp, xs

    for layer in range(DEPTH):
        xp, xs = ffn_pair(xp, xs, layer, 0, 0)
        j = layer // 2
        g1, b1 = row(ln_g[layer, 1]), row(ln_b[layer, 1])
        if layer % 2 == 0:
            w_in, bias, alog = _prep_even(even_w_in[j], fox_b_f[j], gdn_a_log[j], gdn_dt_bias[j])
            common = (w_in, bias, alog, gdn_conv_w[j].astype(F32), row(gdn_norm_g[j]), even_w_out[j].astype(BF16), g1, b1)
            s0 = jnp.zeros((bp, GDN_HEADS, GDN_DK, GDN_DV), F32)
            buf0 = jnp.zeros((bp, CONV_W - 1, GDN_CONV_DIM), F32)
            xp, st_p = _even_layer(xp, None, s0, buf0, *common)
            lc = jnp.pad(cache_fox_logf[j].astype(F32), ((0, 0), (0, 0), (0, LANES - FOX_HEADS)))
            fox_past = (cache_fox_k[j].reshape(bs, past, FOX_W), cache_fox_v[j].reshape(bs, past, FOX_W), lc)
            xs, st_s = _even_layer(xs, fox_past, state_gdn[j].astype(F32), state_gdn_conv[j].astype(F32), *common)
            even_p.append(st_p)
            even_s.append(st_s)
        else:
            bre, bim, cre, cim, sw = _prep_odd(s5_lam_re[j], s5_lam_im[j], s5_log_step[j], s5_b_re[j], s5_b_im[j],
                                               s5_c_re[j], s5_c_im[j])
            common = (odd_w_in[j].astype(BF16), bre, bim, cre, cim, sw, row(s5_d[j]), s5_glu_w[j].astype(BF16),
                      row(s5_glu_b[j]), odd_w_out[j].astype(BF16), g1, b1)
            h0 = jnp.zeros((bp, S5_ROWS, LANES), F32)
            xp, hr, hi = _odd(xp, h0, h0, *common)
            odd_p.append((hr.reshape(bp, S5_GROUPS, S5_STATE), hi.reshape(bp, S5_GROUPS, S5_STATE)))
            xs, hr, hi = _odd(xs, state_s5_re[j].astype(F32).reshape(bs, S5_ROWS, LANES),
                              state_s5_im[j].astype(F32).reshape(bs, S5_ROWS, LANES), *common)
            odd_s.append((hr.reshape(bs, S5_GROUPS, S5_STATE), hi.reshape(bs, S5_GROUPS, S5_STATE)))
        xp, xs = ffn_pair(xp, xs, layer, 1, 2)

    stack = lambda states, idx: jnp.stack([s[idx] for s in states])
    return (xp, xs,
            stack(even_p, 0), stack(even_p, 1), stack(even_p, 2), stack(even_p, 3), stack(even_p, 4),
            stack(odd_p, 0), stack(odd_p, 1),
            stack(even_s, 0), stack(even_s, 1), stack(even_s, 2), stack(even_s, 3), stack(even_s, 4),
            stack(odd_s, 0), stack(odd_s, 1))
```

```python
import functools

import jax
import jax.numpy as jnp
from jax import lax
from jax.experimental import pallas as pl
from jax.experimental.pallas import tpu as pltpu

F32 = jnp.float32
BF16 = jnp.bfloat16

D_MODEL = 1024
DEPTH = 2
CHUNK = 64
FOX_HEADS = 8
FOX_DH = 64
FOX_W = FOX_HEADS * FOX_DH
GDN_HEADS = 4
GDN_DK = 128
GDN_DV = 128
GDN_QK = GDN_HEADS * GDN_DK
GDN_VW = GDN_HEADS * GDN_DV
GDN_CONV_DIM = 2 * GDN_QK + GDN_VW
CONV_W = 4
S5_GROUP = 16
S5_GROUPS = D_MODEL // S5_GROUP
S5_STATE = 64
S5_LANES = S5_GROUPS * S5_STATE
D_FF = ((8 * D_MODEL // 3 + 127) // 128) * 128
ALPHA = (2.0 * DEPTH) ** 0.25
LN_EPS = 1e-5
NORM_EPS = 1e-6
EVEN_SPLITS = (FOX_W, FOX_W, FOX_W, FOX_HEADS, GDN_CONV_DIM, GDN_HEADS, GDN_HEADS, GDN_VW)

LANES = 128
SUBLANES = 8
MXU_DIM = 256
VMEM_LIMIT = 56 * 1024 * 1024

FOX_XW = FOX_HEADS * LANES
_Q0 = 0
_K0 = _Q0 + FOX_W
_V0 = _K0 + FOX_W
_X0 = _V0 + FOX_W
_Z0 = _X0 + GDN_CONV_DIM
_G0 = _Z0 + GDN_VW
EVEN_COLS = _G0 + LANES
_BIAS_LANE = FOX_DH
_ONES_ROW = FOX_DH
_LOGF_LANE, _G_LANE, _BETA_LANE, _CUM_LANE = 0, FOX_HEADS, FOX_HEADS + GDN_HEADS, FOX_HEADS + 2 * GDN_HEADS
NEG_BIG = -1e30
LOG2E = 1.4426950408889634

FFN_ROWS = 1024
FFN_COLS = 256
FFN_PARTS = 4
EVEN_ROWS = 512
ATT_ROWS = 1024
ATT_HEADS = 2
GDN_ROWS = 512
ODD_ROWS = 512
SCAN_UNROLL = 8
S5_ROWS = S5_LANES // LANES
S5_PITCH = S5_ROWS + SUBLANES


def _dot(a, b):
    return jnp.dot(a, b, preferred_element_type=F32)


def _dot_nt(a, b):
    return lax.dot_general(a, b, (((1,), (1,)), ((), ())), preferred_element_type=F32)


def _layer_norm(z, g, b):
    mu = jnp.mean(z, -1, keepdims=True)
    d = z - mu
    var = jnp.mean(d * d, -1, keepdims=True)
    return d * lax.rsqrt(var + LN_EPS) * g + b


def _split3(x):
    hi = x.astype(BF16)
    r = x - hi.astype(F32)
    mid = r.astype(BF16)
    lo = (r - mid.astype(F32)).astype(BF16)
    return hi, mid, lo


def _tri_cumsum(tri, x):
    hi, mid, lo = _split3(x)
    return _dot(tri, hi) + _dot(tri, mid) + _dot(tri, lo)


def _const_spec(shape):
    nd = len(shape)
    return pl.BlockSpec(shape, lambda *_: (0,) * nd, pipeline_mode=pl.Buffered(1))


def _params(sem):
    return pltpu.CompilerParams(dimension_semantics=sem, vmem_limit_bytes=VMEM_LIMIT)


def _ffn_kernel(x_ref, win_ref, wout_ref, g_ref, b_ref, o_ref, h_ref):
    part = x_ref.shape[0] // FFN_PARTS
    halves = [slice(r * part, (r + 1) * part) for r in range(FFN_PARTS)]
    for rows in halves:
        xb = x_ref[rows, :].astype(BF16)
        for c in range(D_FF // FFN_COLS):
            lo = c * FFN_COLS
            a = _dot(xb, win_ref[:, lo:lo + FFN_COLS])
            b = _dot(xb, win_ref[:, D_FF + lo:D_FF + lo + FFN_COLS])
            h_ref[rows, lo:lo + FFN_COLS] = (jax.nn.silu(a) * b).astype(BF16)
    for rows in halves:
        y = _dot(h_ref[rows, :], wout_ref[...])
        o_ref[rows, :] = _layer_norm(ALPHA * x_ref[rows, :] + 0.5 * y, g_ref[...], b_ref[...])


def _ffn(x, w_in, w_out, g, b):
    n = x.shape[0]
    tm = min(FFN_ROWS, n)
    return pl.pallas_call(
        _ffn_kernel,
        grid=(n // tm,),
        in_specs=[pl.BlockSpec((tm, D_MODEL), lambda i: (i, 0)),
                  _const_spec((D_MODEL, 2 * D_FF)),
                  _const_spec((D_FF, D_MODEL)),
                  _const_spec((1, D_MODEL)),
                  _const_spec((1, D_MODEL))],
        out_specs=pl.BlockSpec((tm, D_MODEL), lambda i: (i, 0)),
        out_shape=jax.ShapeDtypeStruct((n, D_MODEL), F32),
        scratch_shapes=[pltpu.VMEM((tm, D_FF), BF16)],
        compiler_params=_params(("parallel",)),
        name="ffn_ln",
    )(x, w_in, w_out, g, b)


def _even_in_kernel(x_ref, w_ref, bias_ref, alog_ref,
                    q_ref, kx_ref, vt_ref, k_ref, v_ref, gx_ref, gz_ref, gates_ref,
                    carry_ref):
    t = pl.program_id(1)
    tm = x_ref.shape[1]

    @pl.when(t == 0)
    def _():
        carry_ref[...] = jnp.zeros_like(carry_ref)

    xb = x_ref[0].astype(BF16)

    def mm(lo, width):
        return _dot(xb, w_ref[:, lo:lo + width])

    lane = lax.broadcasted_iota(jnp.int32, (tm, LANES), 1)
    raw = mm(_G0, LANES) + bias_ref[...]
    logf = jax.nn.log_sigmoid(raw)
    gdec = -jnp.exp(alog_ref[...]) * jax.nn.softplus(raw)
    beta = jax.nn.sigmoid(raw)
    gates = jnp.where(lane < _G_LANE, logf, jnp.where(lane < _BETA_LANE, gdec, jnp.where(lane < _CUM_LANE, beta, logf)))
    row = lax.broadcasted_iota(jnp.int32, (tm, tm), 0)
    col = lax.broadcasted_iota(jnp.int32, (tm, tm), 1)
    tri = jnp.where(row >= col, 1.0, 0.0).astype(BF16)
    cum = _tri_cumsum(tri, logf) + carry_ref[...]
    carry_ref[...] = cum[tm - 1:tm, :]
    gates_ref[0] = jnp.where((lane >= _CUM_LANE) & (lane < _CUM_LANE + FOX_HEADS), cum, gates)

    k = mm(_K0, FOX_W)
    k_ref[0] = k
    v = mm(_V0, FOX_W)
    v_ref[0] = v
    gx_ref[0] = mm(_X0, GDN_CONV_DIM)
    gz_ref[0] = mm(_Z0, GDN_VW)

    c_parts = [part.astype(F32) for part in _split3(cum * LOG2E)]
    q = mm(_Q0, FOX_W) * (FOX_DH ** -0.5 * LOG2E)
    for h in range(FOX_HEADS):
        pair = slice(LANES * (h // 2), LANES * (h // 2 + 1))
        q_h, k_h = q[:, pair], k[:, pair]
        if h % 2:
            q_h, k_h = pltpu.roll(q_h, FOX_DH, 1), pltpu.roll(k_h, FOX_DH, 1)
        ch = [part[:, _CUM_LANE + h:_CUM_LANE + h + 1] for part in c_parts]
        q_ext = jnp.where(lane < _BIAS_LANE + 3, 0.0, jnp.where(lane < _BIAS_LANE + 6, 1.0, 0.0))
        k_ext = jnp.where(lane < _BIAS_LANE + 3, 1.0, 0.0)
        for i in range(3):
            q_ext = jnp.where(lane == _BIAS_LANE + i, ch[i], q_ext)
            k_ext = jnp.where(lane == _BIAS_LANE + 3 + i, -ch[i], k_ext)
        hs = slice(LANES * h, LANES * (h + 1))
        q_ref[0, :, hs] = jnp.where(lane < FOX_DH, q_h, q_ext).astype(BF16)
        kx_ref[0, :, hs] = jnp.where(lane < FOX_DH, k_h, k_ext).astype(BF16)

    v_t = v.T
    pack = 2 * SUBLANES
    ones_blk = jnp.where(lax.broadcasted_iota(jnp.int32, (pack, tm), 0) == 0, 1.0, 0.0).astype(BF16)
    for h in range(FOX_HEADS):
        base = LANES * h
        vt_ref[0, base:base + FOX_DH, :] = v_t[FOX_DH * h:FOX_DH * (h + 1), :].astype(BF16)
        vt_ref[0, base + _ONES_ROW:base + _ONES_ROW + pack, :] = ones_blk
        vt_ref[0, base + _ONES_ROW + pack:base + LANES, :] = jnp.zeros((LANES - _ONES_ROW - pack, tm), BF16)


def _even_in(x, w, bias, alog):
    bsz, t, _ = x.shape
    tm = min(EVEN_ROWS, t)
    row_spec = lambda width: pl.BlockSpec((1, tm, width), lambda b, i: (b, i, 0))
    shp = lambda width, dt: jax.ShapeDtypeStruct((bsz, t, width), dt)
    return pl.pallas_call(
        _even_in_kernel,
        grid=(bsz, t // tm),
        in_specs=[row_spec(D_MODEL), _const_spec((D_MODEL, EVEN_COLS)), _const_spec((1, LANES)), _const_spec((1, LANES))],
        out_specs=[row_spec(FOX_XW), row_spec(FOX_XW),
                   pl.BlockSpec((1, FOX_XW, tm), lambda b, i: (b, 0, i)),
                   row_spec(FOX_W), row_spec(FOX_W), row_spec(GDN_CONV_DIM), row_spec(GDN_VW), row_spec(LANES)],
        out_shape=[shp(FOX_XW, BF16), shp(FOX_XW, BF16), jax.ShapeDtypeStruct((bsz, FOX_XW, t), BF16),
                   shp(FOX_W, F32), shp(FOX_W, F32), shp(GDN_CONV_DIM, F32), shp(GDN_VW, F32), shp(LANES, F32)],
        scratch_shapes=[pltpu.VMEM((1, LANES), F32)],
        compiler_params=_params(("parallel", "arbitrary")),
        name="even_in",
    )(x, w, bias, alog)


def _fox_prompt_kernel(q_ref, k_ref, vt_ref, o_ref, m_ref, acc_ref, s0_ref, s1_ref, p0_ref, p1_ref, x0_ref, x1_ref):
    qi = pl.program_id(2)
    tq = q_ref.shape[1]
    tk = s0_ref.shape[1]
    nh = q_ref.shape[2] // LANES
    heads = [slice(LANES * a, LANES * (a + 1)) for a in range(nh)]
    m_ref[...] = jnp.full(m_ref.shape, NEG_BIG, F32)
    acc_ref[...] = jnp.zeros_like(acc_ref)
    key_pos = lax.broadcasted_iota(jnp.int32, (tk, tq), 0)
    qry_pos = lax.broadcasted_iota(jnp.int32, (tk, tq), 1)

    even = (s0_ref, x0_ref, p0_ref)
    odd = (s1_ref, x1_ref, p1_ref)

    def scores(j, dst, key_shift):
        off = pl.multiple_of(j * tk, tk)
        for a in range(nh):
            s_t = _dot_nt(k_ref[0, pl.ds(off, tk), heads[a]], q_ref[0, :, heads[a]])
            if key_shift is not None:
                s_t = jnp.where(key_pos + key_shift <= qry_pos, s_t, NEG_BIG)
            dst[0][a] = s_t
            dst[1][a] = jnp.max(s_t, axis=0, keepdims=True)

    def weighted_values(j, p_src, a):
        off = pl.multiple_of(j * tk, tk)
        return _dot(vt_ref[0, heads[a], pl.ds(off, tk)], p_src[a])

    def step(j, cur, prev, prefetch, next_shift=None):
        if prefetch:
            scores(j + 1, prev, next_shift)
        j_prev = jnp.maximum(j - 1, 0)
        for a in range(nh):
            m_prev = m_ref[a]
            m_new = jnp.maximum(m_prev, cur[1][a])
            r = weighted_values(j_prev, prev[2], a)
            cur[2][a] = jnp.exp2(cur[0][a] - m_new).astype(BF16)
            acc_ref[a] = jnp.exp2(m_prev - m_new) * (acc_ref[a] + r)
            m_ref[a] = m_new

    p1_ref[...] = jnp.zeros_like(p1_ref)

    @pl.when(qi == 0)
    def _():
        scores(0, even, 0)

    @pl.when(qi > 0)
    def _():
        scores(0, even, None)

        def two_tiles(i, c):
            j = 2 * i
            step(j, even, odd, True)
            step(j + 1, odd, even, True)
            return c

        lax.fori_loop(0, qi - 1, two_tiles, 0)
        step(2 * qi - 2, even, odd, True)
        step(2 * qi - 1, odd, even, True, 0)

    step(2 * qi, even, odd, True, tk)
    step(2 * qi + 1, odd, even, False)

    for a in range(nh):
        acc = acc_ref[a] + weighted_values(2 * qi + 1, p1_ref, a)
        o_ref[0, FOX_DH * a:FOX_DH * (a + 1), :] = (acc[0:FOX_DH] / acc[_ONES_ROW:_ONES_ROW + 1]).astype(BF16)


def _fox_prompt(q, kx, vt):
    bsz, t, _ = q.shape
    tq = min(ATT_ROWS, t)
    nh = ATT_HEADS
    return pl.pallas_call(
        _fox_prompt_kernel,
        grid=(bsz, FOX_HEADS // nh, t // tq),
        in_specs=[pl.BlockSpec((1, tq, nh * LANES), lambda b, p, i: (b, i, p)),
                  pl.BlockSpec((1, t, nh * LANES), lambda b, p, i: (b, 0, p)),
                  pl.BlockSpec((1, nh * LANES, t), lambda b, p, i: (b, p, 0))],
        out_specs=pl.BlockSpec((1, nh * FOX_DH, tq), lambda b, p, i: (b, p, i)),
        out_shape=jax.ShapeDtypeStruct((bsz, FOX_W, t), BF16),
        scratch_shapes=[pltpu.VMEM((nh, 1, tq), F32), pltpu.VMEM((nh, LANES, tq), F32),
                        pltpu.VMEM((nh, tq // 2, tq), F32), pltpu.VMEM((nh, tq // 2, tq), F32),
                        pltpu.VMEM((nh, tq // 2, tq), BF16), pltpu.VMEM((nh, tq // 2, tq), BF16),
                        pltpu.VMEM((nh, 1, tq), F32), pltpu.VMEM((nh, 1, tq), F32)],
        compiler_params=_params(("parallel", "parallel", "arbitrary")),
        name="fox_prompt",
    )(q, kx, vt)


def _fox_sample_kernel(q_ref, kn_ref, vn_ref, gates_ref, kc_ref, vc_ref, lc_ref, o_ref):
    past = kc_ref.shape[1]
    t = q_ref.shape[1]
    row = lax.broadcasted_iota(jnp.int32, (past, past), 0)
    col = lax.broadcasted_iota(jnp.int32, (past, past), 1)
    tri = jnp.where(row >= col, 1.0, 0.0).astype(BF16)
    cum = _tri_cumsum(tri, lc_ref[0])
    ck_cache = (cum - cum[past - 1:past, :]).T
    gts = gates_ref[0]
    ck_new = gts.T
    kc = kc_ref[0].astype(BF16)
    vc = vc_ref[0].astype(BF16)
    kn = kn_ref[0].astype(BF16)
    vn = vn_ref[0].astype(BF16)
    causal = lax.broadcasted_iota(jnp.int32, (t, t), 1) <= lax.broadcasted_iota(jnp.int32, (t, t), 0)
    lane = lax.broadcasted_iota(jnp.int32, (t, LANES), 1)
    outs = []
    for h in range(FOX_HEADS):
        ls = slice(LANES * (h // 2), LANES * (h // 2 + 1))
        qh = q_ref[0, :, LANES * h:LANES * (h + 1)].astype(F32)
        if h % 2 == 0:
            qh = jnp.where(lane < FOX_DH, qh, 0.0).astype(BF16)
        else:
            qh = jnp.where(lane >= FOX_DH, pltpu.roll(qh, FOX_DH, 1), 0.0).astype(BF16)
        cq = gts[:, _CUM_LANE + h:_CUM_LANE + h + 1]
        cq = cq * LOG2E
        s1 = (_dot_nt(qh, kc[:, ls]) + cq) - ck_cache[h:h + 1, :] * LOG2E
        s2 = (_dot_nt(qh, kn[:, ls]) + cq) - ck_new[_CUM_LANE + h:_CUM_LANE + h + 1, :] * LOG2E
        s2 = jnp.where(causal, s2, NEG_BIG)
        m = jnp.maximum(jnp.max(s1, axis=1, keepdims=True), jnp.max(s2, axis=1, keepdims=True))
        p1 = jnp.exp2(s1 - m)
        p2 = jnp.exp2(s2 - m)
        l = jnp.sum(p1, axis=1, keepdims=True) + jnp.sum(p2, axis=1, keepdims=True)
        outs.append((_dot(p1.astype(BF16), vc[:, ls]) + _dot(p2.astype(BF16), vn[:, ls])) / l)
    o = jnp.concatenate([jnp.where(lane < FOX_DH, outs[2 * p], outs[2 * p + 1]) for p in range(FOX_HEADS // 2)], axis=1)
    o_ref[0] = o.T.astype(BF16)


def _fox_sample(q, k, v, gates, past_k, past_v, past_logf):
    bsz, t, _ = k.shape
    past = past_k.shape[1]
    blk = lambda rows, width: pl.BlockSpec((1, rows, width), lambda b: (b, 0, 0))
    return pl.pallas_call(
        _fox_sample_kernel,
        grid=(bsz,),
        in_specs=[blk(t, FOX_XW), blk(t, FOX_W), blk(t, FOX_W), blk(t, LANES),
                  blk(past, FOX_W), blk(past, FOX_W), blk(past, LANES)],
        out_specs=blk(FOX_W, t),
        out_shape=jax.ShapeDtypeStruct((bsz, FOX_W, t), BF16),
        compiler_params=_params(("parallel",)),
        name="fox_sample",
    )(q, k, v, gates, past_k, past_v, past_logf)


def _gdn_kernel(x_ref, gates_ref, z_ref, buf_ref, cw_ref, s0_ref, ng_ref,
                o_ref, sfin_ref, cbuf_ref, xbuf, s_scr, *, chunk):
    t = pl.program_id(1)
    tc = x_ref.shape[1]
    pad = SUBLANES

    @pl.when(t == 0)
    def _():
        xbuf[0:pad, :] = jnp.zeros((pad, GDN_CONV_DIM), F32)
        xbuf[pad - (CONV_W - 1):pad, :] = buf_ref[0]
        s_scr[...] = s0_ref[0]

    @pl.when(t > 0)
    def _():
        xbuf[0:pad, :] = xbuf[tc:tc + pad, :]

    xbuf[pad:pad + tc, :] = x_ref[0]
    cw = cw_ref[...]
    conv = cw[0:1] * xbuf[pad - 3:pad - 3 + tc, :]
    for j in range(1, CONV_W):
        conv = conv + cw[j:j + 1] * xbuf[pad - 3 + j:pad - 3 + j + tc, :]
    cbuf_ref[0] = xbuf[pad + tc - (CONV_W - 1):pad + tc, :]
    act = jax.nn.silu(conv)

    gts = gates_ref[0]
    shift = chunk.bit_length() - 1
    row = lax.broadcasted_iota(jnp.int32, (tc, tc), 0)
    col = lax.broadcasted_iota(jnp.int32, (tc, tc), 1)
    tri = jnp.where((row >= col) & ((row >> shift) == (col >> shift)), 1.0, 0.0).astype(BF16)
    gcum = _tri_cumsum(tri, gts)
    gcum_t = gcum.T
    r_c = lax.broadcasted_iota(jnp.int32, (chunk, chunk), 0)
    c_c = lax.broadcasted_iota(jnp.int32, (chunk, chunk), 1)
    lower = r_c >= c_c
    strict = r_c > c_c
    eye = jnp.where(r_c == c_c, 1.0, 0.0)
    ng = ng_ref[...]

    heads = range(GDN_HEADS)
    chunks = range(tc // chunk)
    rows = [slice(c * chunk, (c + 1) * chunk) for c in chunks]
    cols = [slice(GDN_DK * h, GDN_DK * (h + 1)) for h in heads]

    kn, kn_b, kbeta_b, qn_b, qg, rhs, gc = [], [], [], [], [], [], []
    for h in heads:
        q = act[:, cols[h]]
        k = act[:, GDN_QK + GDN_DK * h:GDN_QK + GDN_DK * (h + 1)]
        v = act[:, 2 * GDN_QK + GDN_DV * h:2 * GDN_QK + GDN_DV * (h + 1)]
        qn = q * lax.rsqrt(jnp.sum(q * q, -1, keepdims=True) + NORM_EPS) * (GDN_DK ** -0.5)
        kn.append(k * lax.rsqrt(jnp.sum(k * k, -1, keepdims=True) + NORM_EPS))
        beta = gts[:, _BETA_LANE + h:_BETA_LANE + h + 1]
        gc.append(gcum[:, _G_LANE + h:_G_LANE + h + 1])
        eg = jnp.exp(gc[h])
        kbeta = kn[h] * beta
        rhs.append(jnp.concatenate([v * beta, kbeta * eg], axis=1).astype(BF16))
        qg.append((qn * eg).astype(BF16))
        kn_b.append(kn[h].astype(BF16))
        kbeta_b.append(kbeta.astype(BF16))
        qn_b.append(qn.astype(BF16))

    units = [(c, h) for c in chunks for h in heads]
    decay, attn, inv, power = {}, {}, {}, {}
    for c, h in units:
        gcc = gc[h][rows[c]]
        gcr = gcum_t[_G_LANE + h:_G_LANE + h + 1, rows[c]]
        decay[c, h] = jnp.where(lower, jnp.exp(jnp.where(lower, gcc - gcr, 0.0)), 0.0)
    for c, h in units:
        power[c, h] = jnp.where(strict, _dot_nt(kbeta_b[h][rows[c]], kn_b[h][rows[c]]) * decay[c, h], 0.0)
        inv[c, h] = eye - power[c, h]
    p = 1
    while 2 * p < chunk:
        for u in units:
            pb = power[u].astype(BF16)
            power[u] = _dot(pb, pb)
        for u in units:
            inv[u] = inv[u] + _dot(inv[u].astype(BF16), power[u].astype(BF16))
        p *= 2
    uw = {}
    for c, h in units:
        uw[c, h] = _dot(inv[c, h].astype(BF16), rhs[h][rows[c]])
    for c, h in units:
        attn[c, h] = (_dot_nt(qn_b[h][rows[c]], kn_b[h][rows[c]]) * decay[c, h]).astype(BF16)

    for c in chunks:
        state = [s_scr[h] for h in heads]
        g_last = [gc[h][rows[c]][chunk - 1:chunk] for h in heads]
        ws = [_dot(jnp.concatenate([uw[c, h][:, GDN_DV:].astype(BF16), qg[h][rows[c]]], axis=0), state[h].astype(BF16))
              for h in heads]
        v_new = [uw[c, h][:, :GDN_DV] - ws[h][:chunk] for h in heads]
        v_new_b = [v_new[h].astype(BF16) for h in heads]
        kdec_t = [(kn[h][rows[c]] * jnp.exp(g_last[h] - gc[h][rows[c]])).T.astype(BF16) for h in heads]
        for h in heads:
            s_scr[h] = state[h] * jnp.exp(g_last[h]) + _dot(kdec_t[h], v_new_b[h])
        for h in heads:
            o = ws[h][chunk:] + _dot(attn[c, h], v_new_b[h])
            z = z_ref[0, rows[c], cols[h]]
            o = o * lax.rsqrt(jnp.mean(o * o, -1, keepdims=True) + NORM_EPS) * ng * jax.nn.silu(z)
            o_ref[0, rows[c], cols[h]] = o.astype(BF16)

    sfin_ref[0] = s_scr[...]


def _gdn(gx, gates, gz, buf, conv_w, s0, norm_g):
    bsz, t, _ = gx.shape
    chunk = min(t, CHUNK)
    tc = min(GDN_ROWS, t)
    row_spec = lambda width: pl.BlockSpec((1, tc, width), lambda b, i: (b, i, 0))
    return pl.pallas_call(
        functools.partial(_gdn_kernel, chunk=chunk),
        grid=(bsz, t // tc),
        in_specs=[row_spec(GDN_CONV_DIM), row_spec(LANES), row_spec(GDN_VW),
                  pl.BlockSpec((1, CONV_W - 1, GDN_CONV_DIM), lambda b, i: (b, 0, 0)),
                  _const_spec((CONV_W, GDN_CONV_DIM)),
                  pl.BlockSpec((1, GDN_HEADS, GDN_DK, GDN_DV), lambda b, i: (b, 0, 0, 0)),
                  _const_spec((1, GDN_DV))],
        out_specs=[row_spec(GDN_VW),
                   pl.BlockSpec((1, GDN_HEADS, GDN_DK, GDN_DV), lambda b, i: (b, 0, 0, 0)),
                   pl.BlockSpec((1, CONV_W - 1, GDN_CONV_DIM), lambda b, i: (b, 0, 0))],
        out_shape=[jax.ShapeDtypeStruct((bsz, t, GDN_VW), BF16),
                   jax.ShapeDtypeStruct((bsz, GDN_HEADS, GDN_DK, GDN_DV), F32),
                   jax.ShapeDtypeStruct((bsz, CONV_W - 1, GDN_CONV_DIM), F32)],
        scratch_shapes=[pltpu.VMEM((tc + SUBLANES, GDN_CONV_DIM), F32),
                        pltpu.VMEM((GDN_HEADS, GDN_DK, GDN_DV), F32)],
        compiler_params=_params(("parallel", "arbitrary")),
        name="gdn",
    )(gx, gates, gz, buf, conv_w, s0, norm_g)


def _even_out_kernel(x_ref, oft_ref, og_ref, w_ref, g_ref, b_ref, o_ref):
    y = lax.dot_general(oft_ref[0], w_ref[0:FOX_W, :], (((0,), (0,)), ((), ())), preferred_element_type=F32)
    y = y + _dot(og_ref[0], w_ref[FOX_W:FOX_W + GDN_VW, :])
    o_ref[0] = _layer_norm(ALPHA * x_ref[0] + y, g_ref[...], b_ref[...])


def _even_out(x, o_fox_t, o_gdn, w, g, b):
    bsz, t, _ = x.shape
    tm = min(EVEN_ROWS, t)
    row_spec = lambda width: pl.BlockSpec((1, tm, width), lambda bb, i: (bb, i, 0))
    return pl.pallas_call(
        _even_out_kernel,
        grid=(bsz, t // tm),
        in_specs=[row_spec(D_MODEL),
                  pl.BlockSpec((1, FOX_W, tm), lambda bb, i: (bb, 0, i)),
                  row_spec(GDN_VW),
                  _const_spec((FOX_W + GDN_VW, D_MODEL)), _const_spec((1, D_MODEL)), _const_spec((1, D_MODEL))],
        out_specs=row_spec(D_MODEL),
        out_shape=jax.ShapeDtypeStruct((bsz, t, D_MODEL), F32),
        compiler_params=_params(("parallel", "parallel")),
        name="even_out",
    )(x, o_fox_t, o_gdn, w, g, b)


def _odd_kernel(x_ref, h0r_ref, h0i_ref, win_ref, bre_ref, bim_ref, cre_ref, cim_ref, lam_ref, d_ref,
                gw_ref, gb_ref, wout_ref, g_ref, b_ref,
                o_ref, hr_out, hi_out, tre, tim, hr_scr, hi_scr):
    t = pl.program_id(1)
    tm = x_ref.shape[1]
    nblk = D_MODEL // MXU_DIM
    per = S5_LANES // nblk

    @pl.when(t == 0)
    def _():
        hr_scr[...] = h0r_ref[0]
        hi_scr[...] = h0i_ref[0]

    x = x_ref[0]
    u = _dot(x.astype(BF16), win_ref[...])
    ub = u.astype(BF16)
    jper = per // LANES

    def column(j):
        return pl.ds(j, tm, stride=S5_PITCH)

    for kk in range(nblk):
        uk = ub[:, MXU_DIM * kk:MXU_DIM * (kk + 1)]
        b_re = _dot(uk, bre_ref[kk])
        b_im = _dot(uk, bim_ref[kk])
        for jj in range(jper):
            tre[column(jper * kk + jj), :] = b_re[:, LANES * jj:LANES * (jj + 1)]
            tim[column(jper * kk + jj), :] = b_im[:, LANES * jj:LANES * (jj + 1)]

    lr = lam_ref[0]
    li = lam_ref[1]

    def step(i, carry):
        hr, hi = carry
        rows = pl.ds(pl.multiple_of(i * S5_PITCH, SUBLANES), S5_ROWS)
        nr = (lr * hr - li * hi) + tre[rows, :]
        ni = (lr * hi + li * hr) + tim[rows, :]
        tre[rows, :] = nr
        tim[rows, :] = ni
        return nr, ni

    hr, hi = lax.fori_loop(0, tm, step, (hr_scr[...], hi_scr[...]), unroll=SCAN_UNROLL)
    hr_scr[...] = hr
    hi_scr[...] = hi
    hr_out[0] = hr
    hi_out[0] = hi

    ys = []
    for kk in range(nblk):
        s_re = jnp.concatenate([tre[column(jper * kk + jj), :] for jj in range(jper)], axis=1).astype(BF16)
        s_im = jnp.concatenate([tim[column(jper * kk + jj), :] for jj in range(jper)], axis=1).astype(BF16)
        ys.append(_dot(s_re, cre_ref[kk]) - _dot(s_im, cim_ref[kk]))
    y = jnp.concatenate(ys, axis=1) + d_ref[...] * u
    zz = jax.nn.gelu(y)
    gated = zz * jax.nn.sigmoid(_dot(zz.astype(BF16), gw_ref[...]) + gb_ref[...])
    out = _dot(gated.astype(BF16), wout_ref[...])
    o_ref[0] = _layer_norm(ALPHA * x + out, g_ref[...], b_ref[...])


def _odd(x, h0r, h0i, w_in, bre, bim, cre, cim, sw, d, gw, gb, w_out, g, b):
    bsz, t, _ = x.shape
    tm = min(ODD_ROWS, t)
    nblk = D_MODEL // MXU_DIM
    per = S5_LANES // nblk
    st_spec = pl.BlockSpec((1, S5_ROWS, LANES), lambda bb, i: (bb, 0, 0))
    return pl.pallas_call(
        _odd_kernel,
        grid=(bsz, t // tm),
        in_specs=[pl.BlockSpec((1, tm, D_MODEL), lambda bb, i: (bb, i, 0)), st_spec, st_spec,
                  _const_spec((D_MODEL, D_MODEL)),
                  _const_spec((nblk, MXU_DIM, per)), _const_spec((nblk, MXU_DIM, per)),
                  _const_spec((nblk, per, MXU_DIM)), _const_spec((nblk, per, MXU_DIM)),
                  _const_spec((2, S5_ROWS, LANES)), _const_spec((1, D_MODEL)),
                  _const_spec((D_MODEL, D_MODEL)), _const_spec((1, D_MODEL)), _const_spec((D_MODEL, D_MODEL)),
                  _const_spec((1, D_MODEL)), _const_spec((1, D_MODEL))],
        out_specs=[pl.BlockSpec((1, tm, D_MODEL), lambda bb, i: (bb, i, 0)), st_spec, st_spec],
        out_shape=[jax.ShapeDtypeStruct((bsz, t, D_MODEL), F32),
                   jax.ShapeDtypeStruct((bsz, S5_ROWS, LANES), F32),
                   jax.ShapeDtypeStruct((bsz, S5_ROWS, LANES), F32)],
        scratch_shapes=[pltpu.VMEM((tm * S5_PITCH, LANES), F32), pltpu.VMEM((tm * S5_PITCH, LANES), F32),
                        pltpu.VMEM((S5_ROWS, LANES), F32), pltpu.VMEM((S5_ROWS, LANES), F32)],
        compiler_params=_params(("parallel", "arbitrary")),
        name="odd_s5",
    )(x, h0r, h0i, w_in, bre, bim, cre, cim, sw, d, gw, gb, w_out, g, b)


def _prep_even(w_in, b_f, a_log, dt_bias):
    offs = [0]
    for s in EVEN_SPLITS:
        offs.append(offs[-1] + s)
    fq, fk, fv, ff, gx, ga, gb, gz = [w_in[:, offs[i]:offs[i + 1]] for i in range(len(EVEN_SPLITS))]

    pad = jnp.zeros((D_MODEL, LANES - (2 * FOX_HEADS + 2 * GDN_HEADS)), w_in.dtype)
    w = jnp.concatenate([fq, fk, fv, gx, gz, ff, ga, gb, ff, pad], axis=1).astype(BF16)
    zpad = jnp.zeros((LANES - (2 * FOX_HEADS + 2 * GDN_HEADS),), F32)
    bias = jnp.concatenate([b_f.astype(F32), dt_bias.astype(F32), jnp.zeros((GDN_HEADS,), F32), b_f.astype(F32), zpad])
    alog = jnp.concatenate([jnp.zeros((FOX_HEADS,), F32), a_log.astype(F32),
                            jnp.zeros((LANES - FOX_HEADS - GDN_HEADS,), F32)])
    return w, bias.reshape(1, LANES), alog.reshape(1, LANES)


def _block_diag(blocks):
    nblk, g, r, c = blocks.shape
    eye = jnp.eye(g, dtype=blocks.dtype)
    return jnp.einsum('kgrc,gh->kgrhc', blocks, eye).reshape(nblk, g * r, g * c)


def _prep_odd(lam_re, lam_im, log_step, b_re, b_im, c_re, c_im):
    lam = lax.complex(lam_re.astype(F32), lam_im.astype(F32))
    lam_bar = jnp.exp(lam * jnp.exp(log_step.astype(F32))[:, None])
    b_bar = ((lam_bar - 1.0) / lam)[..., None] * lax.complex(b_re.astype(F32), b_im.astype(F32))
    nblk = D_MODEL // MXU_DIM
    gper = S5_GROUPS // nblk
    bt = jnp.swapaxes(b_bar, 1, 2).reshape(nblk, gper, S5_GROUP, S5_STATE)
    bre = _block_diag(jnp.real(bt)).astype(BF16)
    bim = _block_diag(jnp.imag(bt)).astype(BF16)
    ct_re = jnp.swapaxes(c_re.astype(F32), 1, 2).reshape(nblk, gper, S5_STATE, S5_GROUP)
    ct_im = jnp.swapaxes(c_im.astype(F32), 1, 2).reshape(nblk, gper, S5_STATE, S5_GROUP)
    cre = _block_diag(ct_re).astype(BF16)
    cim = _block_diag(ct_im).astype(BF16)
    lam_flat = lam_bar.reshape(S5_ROWS, LANES)
    lam_ri = jnp.stack([jnp.real(lam_flat), jnp.imag(lam_flat)]).astype(F32)
    return bre, bim, cre, cim, lam_ri


def _even_layer(x, fox_past, s0, buf0, w_in, bias, alog, conv_w, norm_g, w_out, g, b):
    bsz, t, _ = x.shape
    q, kx, vt, k, v, gx, gz, gates = _even_in(x, w_in, bias, alog)
    if fox_past is None:
        o_fox_t = _fox_prompt(q, kx, vt)
    else:
        o_fox_t = _fox_sample(q, k, v, gates, *fox_past)
    o_gdn, s_new, new_buf = _gdn(gx, gates, gz, buf0, conv_w, s0, norm_g)
    y = _even_out(x, o_fox_t, o_gdn, w_out, g, b)
    state = (k.reshape(bsz, t, FOX_HEADS, FOX_DH), v.reshape(bsz, t, FOX_HEADS, FOX_DH),
             gates[:, :, :FOX_HEADS], s_new, new_buf)
    return y, state


def kernel(x_prompt, x_sample, cache_fox_k, cache_fox_v, cache_fox_logf, state_gdn, state_gdn_conv, state_s5_re, state_s5_im, ffn_w_in, ffn_w_out, ln_g, ln_b, even_w_in, fox_b_f, gdn_conv_w, gdn_a_log, gdn_dt_bias, gdn_norm_g, even_w_out, odd_w_in, s5_lam_re, s5_lam_im, s5_log_step, s5_b_re, s5_b_im, s5_c_re, s5_c_im, s5_d, s5_glu_w, s5_glu_b, odd_w_out):
    bp, tp, _ = x_prompt.shape
    bs, ts, _ = x_sample.shape
    past = cache_fox_k.shape[2]
    xp, xs = x_prompt, x_sample
    even_p, even_s, odd_p, odd_s = [], [], [], []
    row = lambda a: a.astype(F32).reshape(1, -1)
    ffn_w_in_b = ffn_w_in.astype(BF16)
    ffn_w_out_b = ffn_w_out.astype(BF16)

    def ffn_pair(xp, xs, layer, idx, ln_idx):
        w_in = ffn_w_in_b[layer, idx]
        w_out = ffn_w_out_b[layer, idx]
        g, b = row(ln_g[layer, ln_idx]), row(ln_b[layer, ln_idx])
        xp = _ffn(xp.reshape(bp * tp, D_MODEL), w_in, w_out, g, b).reshape(bp, tp, D_MODEL)
        xs = _ffn(xs.reshape(bs * ts, D_MODEL), w_in, w_out, g, b).reshape(bs, ts, D_MODEL)
        return xp, xs

    for layer in range(DEPTH):
        xp, xs = ffn_pair(xp, xs, layer, 0, 0)
        j = layer // 2
        g1, b1 = row(ln_g[layer, 1]), row(ln_b[layer, 1])
        if layer % 2 == 0:
            w_in, bias, alog = _prep_even(even_w_in[j], fox_b_f[j], gdn_a_log[j], gdn_dt_bias[j])
            common = (w_in, bias, alog, gdn_conv_w[j].astype(F32), row(gdn_norm_g[j]), even_w_out[j].astype(BF16), g1, b1)
            s0 = jnp.zeros((bp, GDN_HEADS, GDN_DK, GDN_DV), F32)
            buf0 = jnp.zeros((bp, CONV_W - 1, GDN_CONV_DIM), F32)
            xp, st_p = _even_layer(xp, None, s0, buf0, *common)
            lc = jnp.pad(cache_fox_logf[j].astype(F32), ((0, 0), (0, 0), (0, LANES - FOX_HEADS)))
            fox_past = (cache_fox_k[j].reshape(bs, past, FOX_W), cache_fox_v[j].reshape(bs, past, FOX_W), lc)
            xs, st_s = _even_layer(xs, fox_past, state_gdn[j].astype(F32), state_gdn_conv[j].astype(F32), *common)
            even_p.append(st_p)
            even_s.append(st_s)
        else:
            bre, bim, cre, cim, sw = _prep_odd(s5_lam_re[j], s5_lam_im[j], s5_log_step[j], s5_b_re[j], s5_b_im[j],
                                               s5_c_re[j], s5_c_im[j])
            common = (odd_w_in[j].astype(BF16), bre, bim, cre, cim, sw, row(s5_d[j]), s5_glu_w[j].astype(BF16),
                      row(s5_glu_b[j]), odd_w_out[j].astype(BF16), g1, b1)
            h0 = jnp.zeros((bp, S5_ROWS, LANES), F32)
            xp, hr, hi = _odd(xp, h0, h0, *common)
            odd_p.append((hr.reshape(bp, S5_GROUPS, S5_STATE), hi.reshape(bp, S5_GROUPS, S5_STATE)))
            xs, hr, hi = _odd(xs, state_s5_re[j].astype(F32).reshape(bs, S5_ROWS, LANES),
                              state_s5_im[j].astype(F32).reshape(bs, S5_ROWS, LANES), *common)
            odd_s.append((hr.reshape(bs, S5_GROUPS, S5_STATE), hi.reshape(bs, S5_GROUPS, S5_STATE)))
        xp, xs = ffn_pair(xp, xs, layer, 1, 2)

    stack = lambda states, idx: jnp.stack([s[idx] for s in states])
    return (xp, xs,
            stack(even_p, 0), stack(even_p, 1), stack(even_p, 2), stack(even_p, 3), stack(even_p, 4),
            stack(odd_p, 0), stack(odd_p, 1),
            stack(even_s, 0), stack(even_s, 1), stack(even_s, 2), stack(even_s, 3), stack(even_s, 4),
            stack(odd_s, 0), stack(odd_s, 1))
```

```python
import functools

import jax
import jax.numpy as jnp
from jax import lax
from jax.experimental import pallas as pl
from jax.experimental.pallas import tpu as pltpu

F32 = jnp.float32
BF16 = jnp.bfloat16

D_MODEL = 1024
DEPTH = 2
CHUNK = 64
FOX_HEADS = 8
FOX_DH = 64
FOX_W = FOX_HEADS * FOX_DH
GDN_HEADS = 4
GDN_DK = 128
GDN_DV = 128
GDN_QK = GDN_HEADS * GDN_DK
GDN_VW = GDN_HEADS * GDN_DV
GDN_CONV_DIM = 2 * GDN_QK + GDN_VW
CONV_W = 4
S5_GROUP = 16
S5_GROUPS = D_MODEL // S5_GROUP
S5_STATE = 64
S5_LANES = S5_GROUPS * S5_STATE
D_FF = ((8 * D_MODEL // 3 + 127) // 128) * 128
ALPHA = (2.0 * DEPTH) ** 0.25
LN_EPS = 1e-5
NORM_EPS = 1e-6
EVEN_SPLITS = (FOX_W, FOX_W, FOX_W, FOX_HEADS, GDN_CONV_DIM, GDN_HEADS, GDN_HEADS, GDN_VW)

LANES = 128
SUBLANES = 8
MXU_DIM = 256
VMEM_LIMIT = 56 * 1024 * 1024

FOX_XW = FOX_HEADS * LANES
_Q0 = 0
_K0 = _Q0 + FOX_W
_V0 = _K0 + FOX_W
_X0 = _V0 + FOX_W
_Z0 = _X0 + GDN_CONV_DIM
_G0 = _Z0 + GDN_VW
EVEN_COLS = _G0 + LANES
_BIAS_LANE = FOX_DH
_ONES_ROW = FOX_DH
_LOGF_LANE, _G_LANE, _BETA_LANE, _CUM_LANE = 0, FOX_HEADS, FOX_HEADS + GDN_HEADS, FOX_HEADS + 2 * GDN_HEADS
NEG_BIG = -1e30
LOG2E = 1.4426950408889634

FFN_ROWS = 1024
FFN_COLS = 256
PART_ROWS = 256
EVEN_ROWS = 512
ATT_ROWS = 1024
ATT_HEADS = 2
GDN_ROWS = 512
ODD_ROWS = 512
SCAN_UNROLL = 8
S5_ROWS = S5_LANES // LANES
S5_PITCH = S5_ROWS + SUBLANES


def _dot(a, b):
    return jnp.dot(a, b, preferred_element_type=F32)


def _dot_nt(a, b):
    return lax.dot_general(a, b, (((1,), (1,)), ((), ())), preferred_element_type=F32)


def _layer_norm(z, g, b):
    mu = jnp.mean(z, -1, keepdims=True)
    d = z - mu
    var = jnp.mean(d * d, -1, keepdims=True)
    return d * lax.rsqrt(var + LN_EPS) * g + b


def _split3(x):
    hi = x.astype(BF16)
    r = x - hi.astype(F32)
    mid = r.astype(BF16)
    lo = (r - mid.astype(F32)).astype(BF16)
    return hi, mid, lo


def _tri_cumsum(tri, x):
    hi, mid, lo = _split3(x)
    return _dot(tri, hi) + _dot(tri, mid) + _dot(tri, lo)


def _row_parts(rows):
    n = max(1, rows // PART_ROWS)
    return [slice(r * (rows // n), (r + 1) * (rows // n)) for r in range(n)]


def _const_spec(shape):
    nd = len(shape)
    return pl.BlockSpec(shape, lambda *_: (0,) * nd, pipeline_mode=pl.Buffered(1))


def _params(sem):
    return pltpu.CompilerParams(dimension_semantics=sem, vmem_limit_bytes=VMEM_LIMIT)


def _ffn_kernel(x_ref, win_ref, wout_ref, g_ref, b_ref, o_ref, h_ref):
    halves = _row_parts(x_ref.shape[0])
    for rows in halves:
        xb = x_ref[rows, :].astype(BF16)
        for c in range(D_FF // FFN_COLS):
            lo = c * FFN_COLS
            a = _dot(xb, win_ref[:, lo:lo + FFN_COLS])
            b = _dot(xb, win_ref[:, D_FF + lo:D_FF + lo + FFN_COLS])
            h_ref[rows, lo:lo + FFN_COLS] = (jax.nn.silu(a) * b).astype(BF16)
    for rows in halves:
        y = _dot(h_ref[rows, :], wout_ref[...])
        o_ref[rows, :] = _layer_norm(ALPHA * x_ref[rows, :] + 0.5 * y, g_ref[...], b_ref[...])


def _ffn(x, w_in, w_out, g, b):
    n = x.shape[0]
    tm = min(FFN_ROWS, n)
    return pl.pallas_call(
        _ffn_kernel,
        grid=(n // tm,),
        in_specs=[pl.BlockSpec((tm, D_MODEL), lambda i: (i, 0)),
                  _const_spec((D_MODEL, 2 * D_FF)),
                  _const_spec((D_FF, D_MODEL)),
                  _const_spec((1, D_MODEL)),
                  _const_spec((1, D_MODEL))],
        out_specs=pl.BlockSpec((tm, D_MODEL), lambda i: (i, 0)),
        out_shape=jax.ShapeDtypeStruct((n, D_MODEL), F32),
        scratch_shapes=[pltpu.VMEM((tm, D_FF), BF16)],
        compiler_params=_params(("parallel",)),
        name="ffn_ln",
    )(x, w_in, w_out, g, b)


def _even_in_kernel(x_ref, w_ref, bias_ref, alog_ref,
                    q_ref, kx_ref, vt_ref, k_ref, v_ref, gx_ref, gz_ref, gates_ref,
                    carry_ref):
    t = pl.program_id(1)
    tm = x_ref.shape[1]

    @pl.when(t == 0)
    def _():
        carry_ref[...] = jnp.zeros_like(carry_ref)

    xb = x_ref[0].astype(BF16)

    def mm(lo, width):
        return _dot(xb, w_ref[:, lo:lo + width])

    lane = lax.broadcasted_iota(jnp.int32, (tm, LANES), 1)
    raw = mm(_G0, LANES) + bias_ref[...]
    logf = jax.nn.log_sigmoid(raw)
    gdec = -jnp.exp(alog_ref[...]) * jax.nn.softplus(raw)
    beta = jax.nn.sigmoid(raw)
    gates = jnp.where(lane < _G_LANE, logf, jnp.where(lane < _BETA_LANE, gdec, jnp.where(lane < _CUM_LANE, beta, logf)))
    row = lax.broadcasted_iota(jnp.int32, (tm, tm), 0)
    col = lax.broadcasted_iota(jnp.int32, (tm, tm), 1)
    tri = jnp.where(row >= col, 1.0, 0.0).astype(BF16)
    cum = _tri_cumsum(tri, logf) + carry_ref[...]
    carry_ref[...] = cum[tm - 1:tm, :]
    gates_ref[0] = jnp.where((lane >= _CUM_LANE) & (lane < _CUM_LANE + FOX_HEADS), cum, gates)

    k = mm(_K0, FOX_W)
    k_ref[0] = k
    v = mm(_V0, FOX_W)
    v_ref[0] = v
    gx_ref[0] = mm(_X0, GDN_CONV_DIM)
    gz_ref[0] = mm(_Z0, GDN_VW)

    c_parts = [part.astype(F32) for part in _split3(cum * LOG2E)]
    q = mm(_Q0, FOX_W) * (FOX_DH ** -0.5 * LOG2E)
    for h in range(FOX_HEADS):
        pair = slice(LANES * (h // 2), LANES * (h // 2 + 1))
        q_h, k_h = q[:, pair], k[:, pair]
        if h % 2:
            q_h, k_h = pltpu.roll(q_h, FOX_DH, 1), pltpu.roll(k_h, FOX_DH, 1)
        ch = [part[:, _CUM_LANE + h:_CUM_LANE + h + 1] for part in c_parts]
        q_ext = jnp.where(lane < _BIAS_LANE + 3, 0.0, jnp.where(lane < _BIAS_LANE + 6, 1.0, 0.0))
        k_ext = jnp.where(lane < _BIAS_LANE + 3, 1.0, 0.0)
        for i in range(3):
            q_ext = jnp.where(lane == _BIAS_LANE + i, ch[i], q_ext)
            k_ext = jnp.where(lane == _BIAS_LANE + 3 + i, -ch[i], k_ext)
        hs = slice(LANES * h, LANES * (h + 1))
        q_ref[0, :, hs] = jnp.where(lane < FOX_DH, q_h, q_ext).astype(BF16)
        kx_ref[0, :, hs] = jnp.where(lane < FOX_DH, k_h, k_ext).astype(BF16)

    v_t = v.T
    pack = 2 * SUBLANES
    ones_blk = jnp.where(lax.broadcasted_iota(jnp.int32, (pack, tm), 0) == 0, 1.0, 0.0).astype(BF16)
    for h in range(FOX_HEADS):
        base = LANES * h
        vt_ref[0, base:base + FOX_DH, :] = v_t[FOX_DH * h:FOX_DH * (h + 1), :].astype(BF16)
        vt_ref[0, base + _ONES_ROW:base + _ONES_ROW + pack, :] = ones_blk
        vt_ref[0, base + _ONES_ROW + pack:base + LANES, :] = jnp.zeros((LANES - _ONES_ROW - pack, tm), BF16)


def _even_in(x, w, bias, alog):
    bsz, t, _ = x.shape
    tm = min(EVEN_ROWS, t)
    row_spec = lambda width: pl.BlockSpec((1, tm, width), lambda b, i: (b, i, 0))
    shp = lambda width, dt: jax.ShapeDtypeStruct((bsz, t, width), dt)
    return pl.pallas_call(
        _even_in_kernel,
        grid=(bsz, t // tm),
        in_specs=[row_spec(D_MODEL), _const_spec((D_MODEL, EVEN_COLS)), _const_spec((1, LANES)), _const_spec((1, LANES))],
        out_specs=[row_spec(FOX_XW), row_spec(FOX_XW),
                   pl.BlockSpec((1, FOX_XW, tm), lambda b, i: (b, 0, i)),
                   row_spec(FOX_W), row_spec(FOX_W), row_spec(GDN_CONV_DIM), row_spec(GDN_VW), row_spec(LANES)],
        out_shape=[shp(FOX_XW, BF16), shp(FOX_XW, BF16), jax.ShapeDtypeStruct((bsz, FOX_XW, t), BF16),
                   shp(FOX_W, F32), shp(FOX_W, F32), shp(GDN_CONV_DIM, F32), shp(GDN_VW, F32), shp(LANES, F32)],
        scratch_shapes=[pltpu.VMEM((1, LANES), F32)],
        compiler_params=_params(("parallel", "arbitrary")),
        name="even_in",
    )(x, w, bias, alog)


def _fox_prompt_kernel(q_ref, k_ref, vt_ref, o_ref, m_ref, acc_ref, s0_ref, s1_ref, p0_ref, p1_ref, x0_ref, x1_ref):
    qi = pl.program_id(2)
    tq = q_ref.shape[1]
    tk = s0_ref.shape[1]
    nh = q_ref.shape[2] // LANES
    heads = [slice(LANES * a, LANES * (a + 1)) for a in range(nh)]
    m_ref[...] = jnp.full(m_ref.shape, NEG_BIG, F32)
    acc_ref[...] = jnp.zeros_like(acc_ref)
    key_pos = lax.broadcasted_iota(jnp.int32, (tk, tq), 0)
    qry_pos = lax.broadcasted_iota(jnp.int32, (tk, tq), 1)

    upper = slice(tk, tq)
    visible_upper = (lax.broadcasted_iota(jnp.int32, (tk, tq - tk), 0)
                     <= lax.broadcasted_iota(jnp.int32, (tk, tq - tk), 1))
    even = (s0_ref, x0_ref, p0_ref)
    odd = (s1_ref, x1_ref, p1_ref)

    def scores(j, dst, key_shift):
        off = pl.multiple_of(j * tk, tk)
        for a in range(nh):
            if key_shift == tk:
                s_t = _dot_nt(k_ref[0, pl.ds(off, tk), heads[a]], q_ref[0, tk:tq, heads[a]])
                s_t = jnp.where(visible_upper, s_t, NEG_BIG)
                dst[0][a, :, upper] = s_t
                dst[1][a, :, upper] = jnp.broadcast_to(jnp.max(s_t, axis=0, keepdims=True), (SUBLANES, tq - tk))
                continue
            s_t = _dot_nt(k_ref[0, pl.ds(off, tk), heads[a]], q_ref[0, :, heads[a]])
            if key_shift is not None:
                s_t = jnp.where(key_pos + key_shift <= qry_pos, s_t, NEG_BIG)
            dst[0][a] = s_t
            dst[1][a] = jnp.broadcast_to(jnp.max(s_t, axis=0, keepdims=True), (SUBLANES, tq))

    def weighted_values(j, p_src, a):
        off = pl.multiple_of(j * tk, tk)
        return _dot(vt_ref[0, heads[a], pl.ds(off, tk)], p_src[a])

    def step(j, cur, prev, prefetch, next_shift=None):
        if prefetch:
            scores(j + 1, prev, next_shift)
        j_prev = jnp.maximum(j - 1, 0)
        for a in range(nh):
            m_prev = m_ref[a]
            m_new = jnp.maximum(m_prev, cur[1][a])
            r = weighted_values(j_prev, prev[2], a)
            cur[2][a] = jnp.exp2(cur[0][a] - m_new[0:1]).astype(BF16)
            acc_ref[a] = jnp.exp2(m_prev[0:1] - m_new[0:1]) * (acc_ref[a] + r)
            m_ref[a] = m_new

    p1_ref[...] = jnp.zeros_like(p1_ref)

    @pl.when(qi == 0)
    def _():
        scores(0, even, 0)

    @pl.when(qi > 0)
    def _():
        scores(0, even, None)

        def two_tiles(i, c):
            j = 2 * i
            step(j, even, odd, True)
            step(j + 1, odd, even, True)
            return c

        lax.fori_loop(0, qi - 1, two_tiles, 0)
        step(2 * qi - 2, even, odd, True)
        step(2 * qi - 1, odd, even, True, 0)

    step(2 * qi, even, odd, True, tk)

    j_last = 2 * qi + 1
    off_last = pl.multiple_of(j_last * tk, tk)
    for a in range(nh):
        out_rows = slice(FOX_DH * a, FOX_DH * (a + 1))
        r = weighted_values(j_last - 1, p0_ref, a)
        m_prev = m_ref[a, :, upper]
        m_new = jnp.maximum(m_prev, x1_ref[a, :, upper])
        p_t = jnp.exp2(s1_ref[a, :, upper] - m_new[0:1]).astype(BF16)
        lo = acc_ref[a, :, 0:tk] + r[:, 0:tk]
        hi = jnp.exp2(m_prev[0:1] - m_new[0:1]) * (acc_ref[a, :, upper] + r[:, upper])
        hi = hi + _dot(vt_ref[0, heads[a], pl.ds(off_last, tk)], p_t)
        o_ref[0, out_rows, 0:tk] = (lo[0:FOX_DH] / lo[_ONES_ROW:_ONES_ROW + 1]).astype(BF16)
        o_ref[0, out_rows, upper] = (hi[0:FOX_DH] / hi[_ONES_ROW:_ONES_ROW + 1]).astype(BF16)


def _fox_prompt(q, kx, vt):
    bsz, t, _ = q.shape
    tq = min(ATT_ROWS, t)
    nh = ATT_HEADS
    return pl.pallas_call(
        _fox_prompt_kernel,
        grid=(bsz, FOX_HEADS // nh, t // tq),
        in_specs=[pl.BlockSpec((1, tq, nh * LANES), lambda b, p, i: (b, i, p)),
                  pl.BlockSpec((1, t, nh * LANES), lambda b, p, i: (b, 0, p)),
                  pl.BlockSpec((1, nh * LANES, t), lambda b, p, i: (b, p, 0))],
        out_specs=pl.BlockSpec((1, nh * FOX_DH, tq), lambda b, p, i: (b, p, i)),
        out_shape=jax.ShapeDtypeStruct((bsz, FOX_W, t), BF16),
        scratch_shapes=[pltpu.VMEM((nh, SUBLANES, tq), F32), pltpu.VMEM((nh, LANES, tq), F32),
                        pltpu.VMEM((nh, tq // 2, tq), F32), pltpu.VMEM((nh, tq // 2, tq), F32),
                        pltpu.VMEM((nh, tq // 2, tq), BF16), pltpu.VMEM((nh, tq // 2, tq), BF16),
                        pltpu.VMEM((nh, SUBLANES, tq), F32), pltpu.VMEM((nh, SUBLANES, tq), F32)],
        compiler_params=_params(("parallel", "parallel", "arbitrary")),
        name="fox_prompt",
    )(q, kx, vt)


def _fox_sample_kernel(q_ref, kn_ref, vn_ref, gates_ref, kc_ref, vc_ref, lc_ref, o_ref):
    past = kc_ref.shape[1]
    t = q_ref.shape[1]
    row = lax.broadcasted_iota(jnp.int32, (past, past), 0)
    col = lax.broadcasted_iota(jnp.int32, (past, past), 1)
    tri = jnp.where(row >= col, 1.0, 0.0).astype(BF16)
    cum = _tri_cumsum(tri, lc_ref[0])
    ck_cache = (cum - cum[past - 1:past, :]).T
    gts = gates_ref[0]
    ck_new = gts.T
    kc = kc_ref[0].astype(BF16)
    vc = vc_ref[0].astype(BF16)
    kn = kn_ref[0].astype(BF16)
    vn = vn_ref[0].astype(BF16)
    causal = lax.broadcasted_iota(jnp.int32, (t, t), 1) <= lax.broadcasted_iota(jnp.int32, (t, t), 0)
    lane = lax.broadcasted_iota(jnp.int32, (t, LANES), 1)
    outs = []
    for h in range(FOX_HEADS):
        ls = slice(LANES * (h // 2), LANES * (h // 2 + 1))
        qh = q_ref[0, :, LANES * h:LANES * (h + 1)].astype(F32)
        if h % 2 == 0:
            qh = jnp.where(lane < FOX_DH, qh, 0.0).astype(BF16)
        else:
            qh = jnp.where(lane >= FOX_DH, pltpu.roll(qh, FOX_DH, 1), 0.0).astype(BF16)
        cq = gts[:, _CUM_LANE + h:_CUM_LANE + h + 1]
        cq = cq * LOG2E
        s1 = (_dot_nt(qh, kc[:, ls]) + cq) - ck_cache[h:h + 1, :] * LOG2E
        s2 = (_dot_nt(qh, kn[:, ls]) + cq) - ck_new[_CUM_LANE + h:_CUM_LANE + h + 1, :] * LOG2E
        s2 = jnp.where(causal, s2, NEG_BIG)
        m = jnp.maximum(jnp.max(s1, axis=1, keepdims=True), jnp.max(s2, axis=1, keepdims=True))
        p1 = jnp.exp2(s1 - m)
        p2 = jnp.exp2(s2 - m)
        l = jnp.sum(p1, axis=1, keepdims=True) + jnp.sum(p2, axis=1, keepdims=True)
        outs.append((_dot(p1.astype(BF16), vc[:, ls]) + _dot(p2.astype(BF16), vn[:, ls])) / l)
    o = jnp.concatenate([jnp.where(lane < FOX_DH, outs[2 * p], outs[2 * p + 1]) for p in range(FOX_HEADS // 2)], axis=1)
    o_ref[0] = o.T.astype(BF16)


def _fox_sample(q, k, v, gates, past_k, past_v, past_logf):
    bsz, t, _ = k.shape
    past = past_k.shape[1]
    blk = lambda rows, width: pl.BlockSpec((1, rows, width), lambda b: (b, 0, 0))
    return pl.pallas_call(
        _fox_sample_kernel,
        grid=(bsz,),
        in_specs=[blk(t, FOX_XW), blk(t, FOX_W), blk(t, FOX_W), blk(t, LANES),
                  blk(past, FOX_W), blk(past, FOX_W), blk(past, LANES)],
        out_specs=blk(FOX_W, t),
        out_shape=jax.ShapeDtypeStruct((bsz, FOX_W, t), BF16),
        compiler_params=_params(("parallel",)),
        name="fox_sample",
    )(q, k, v, gates, past_k, past_v, past_logf)


def _gdn_kernel(x_ref, gates_ref, z_ref, buf_ref, cw_ref, s0_ref, ng_ref,
                o_ref, sfin_ref, cbuf_ref, xbuf, s_scr, *, chunk):
    t = pl.program_id(1)
    tc = x_ref.shape[1]
    pad = SUBLANES

    @pl.when(t == 0)
    def _():
        xbuf[0:pad, :] = jnp.zeros((pad, GDN_CONV_DIM), F32)
        xbuf[pad - (CONV_W - 1):pad, :] = buf_ref[0]
        s_scr[...] = s0_ref[0]

    @pl.when(t > 0)
    def _():
        xbuf[0:pad, :] = xbuf[tc:tc + pad, :]

    xbuf[pad:pad + tc, :] = x_ref[0]
    cw = cw_ref[...]
    conv = cw[0:1] * xbuf[pad - 3:pad - 3 + tc, :]
    for j in range(1, CONV_W):
        conv = conv + cw[j:j + 1] * xbuf[pad - 3 + j:pad - 3 + j + tc, :]
    cbuf_ref[0] = xbuf[pad + tc - (CONV_W - 1):pad + tc, :]
    act = jax.nn.silu(conv)

    gts = gates_ref[0]
    shift = chunk.bit_length() - 1
    row = lax.broadcasted_iota(jnp.int32, (tc, tc), 0)
    col = lax.broadcasted_iota(jnp.int32, (tc, tc), 1)
    tri = jnp.where((row >= col) & ((row >> shift) == (col >> shift)), 1.0, 0.0).astype(BF16)
    gcum = _tri_cumsum(tri, gts)
    gcum_t = gcum.T
    r_c = lax.broadcasted_iota(jnp.int32, (chunk, chunk), 0)
    c_c = lax.broadcasted_iota(jnp.int32, (chunk, chunk), 1)
    lower = r_c >= c_c
    strict = r_c > c_c
    eye = jnp.where(r_c == c_c, 1.0, 0.0)
    ng = ng_ref[...]

    heads = range(GDN_HEADS)
    chunks = range(tc // chunk)
    rows = [slice(c * chunk, (c + 1) * chunk) for c in chunks]
    cols = [slice(GDN_DK * h, GDN_DK * (h + 1)) for h in heads]

    kn, kn_b, kbeta_b, qn_b, qg, rhs, gc = [], [], [], [], [], [], []
    for h in heads:
        q = act[:, cols[h]]
        k = act[:, GDN_QK + GDN_DK * h:GDN_QK + GDN_DK * (h + 1)]
        v = act[:, 2 * GDN_QK + GDN_DV * h:2 * GDN_QK + GDN_DV * (h + 1)]
        qn = q * lax.rsqrt(jnp.sum(q * q, -1, keepdims=True) + NORM_EPS) * (GDN_DK ** -0.5)
        kn.append(k * lax.rsqrt(jnp.sum(k * k, -1, keepdims=True) + NORM_EPS))
        beta = gts[:, _BETA_LANE + h:_BETA_LANE + h + 1]
        gc.append(gcum[:, _G_LANE + h:_G_LANE + h + 1])
        eg = jnp.exp(gc[h])
        kbeta = kn[h] * beta
        rhs.append(jnp.concatenate([v * beta, kbeta * eg], axis=1).astype(BF16))
        qg.append((qn * eg).astype(BF16))
        kn_b.append(kn[h].astype(BF16))
        kbeta_b.append(kbeta.astype(BF16))
        qn_b.append(qn.astype(BF16))

    units = [(c, h) for c in chunks for h in heads]
    decay, attn, inv, power = {}, {}, {}, {}
    for c, h in units:
        gcc = gc[h][rows[c]]
        gcr = gcum_t[_G_LANE + h:_G_LANE + h + 1, rows[c]]
        decay[c, h] = jnp.where(lower, jnp.exp(jnp.where(lower, gcc - gcr, 0.0)), 0.0)
    for c, h in units:
        power[c, h] = jnp.where(strict, _dot_nt(kbeta_b[h][rows[c]], kn_b[h][rows[c]]) * decay[c, h], 0.0)
        inv[c, h] = eye - power[c, h]
    p = 1
    while 2 * p < chunk:
        for u in units:
            pb = power[u].astype(BF16)
            power[u] = _dot(pb, pb)
        for u in units:
            inv[u] = inv[u] + _dot(inv[u].astype(BF16), power[u].astype(BF16))
        p *= 2
    uw = {}
    for c, h in units:
        uw[c, h] = _dot(inv[c, h].astype(BF16), rhs[h][rows[c]])
    for c, h in units:
        attn[c, h] = (_dot_nt(qn_b[h][rows[c]], kn_b[h][rows[c]]) * decay[c, h]).astype(BF16)

    for c in chunks:
        state = [s_scr[h] for h in heads]
        g_last = [gc[h][rows[c]][chunk - 1:chunk] for h in heads]
        ws = [_dot(jnp.concatenate([uw[c, h][:, GDN_DV:].astype(BF16), qg[h][rows[c]]], axis=0), state[h].astype(BF16))
              for h in heads]
        v_new = [uw[c, h][:, :GDN_DV] - ws[h][:chunk] for h in heads]
        v_new_b = [v_new[h].astype(BF16) for h in heads]
        kdec_t = [(kn[h][rows[c]] * jnp.exp(g_last[h] - gc[h][rows[c]])).T.astype(BF16) for h in heads]
        for h in heads:
            s_scr[h] = state[h] * jnp.exp(g_last[h]) + _dot(kdec_t[h], v_new_b[h])
        for h in heads:
            o = ws[h][chunk:] + _dot(attn[c, h], v_new_b[h])
            z = z_ref[0, rows[c], cols[h]]
            o = o * lax.rsqrt(jnp.mean(o * o, -1, keepdims=True) + NORM_EPS) * ng * jax.nn.silu(z)
            o_ref[0, rows[c], cols[h]] = o.astype(BF16)

    sfin_ref[0] = s_scr[...]


def _gdn(gx, gates, gz, buf, conv_w, s0, norm_g):
    bsz, t, _ = gx.shape
    chunk = min(t, CHUNK)
    tc = min(GDN_ROWS, t)
    row_spec = lambda width: pl.BlockSpec((1, tc, width), lambda b, i: (b, i, 0))
    return pl.pallas_call(
        functools.partial(_gdn_kernel, chunk=chunk),
        grid=(bsz, t // tc),
        in_specs=[row_spec(GDN_CONV_DIM), row_spec(LANES), row_spec(GDN_VW),
                  pl.BlockSpec((1, CONV_W - 1, GDN_CONV_DIM), lambda b, i: (b, 0, 0)),
                  _const_spec((CONV_W, GDN_CONV_DIM)),
                  pl.BlockSpec((1, GDN_HEADS, GDN_DK, GDN_DV), lambda b, i: (b, 0, 0, 0)),
                  _const_spec((1, GDN_DV))],
        out_specs=[row_spec(GDN_VW),
                   pl.BlockSpec((1, GDN_HEADS, GDN_DK, GDN_DV), lambda b, i: (b, 0, 0, 0)),
                   pl.BlockSpec((1, CONV_W - 1, GDN_CONV_DIM), lambda b, i: (b, 0, 0))],
        out_shape=[jax.ShapeDtypeStruct((bsz, t, GDN_VW), BF16),
                   jax.ShapeDtypeStruct((bsz, GDN_HEADS, GDN_DK, GDN_DV), F32),
                   jax.ShapeDtypeStruct((bsz, CONV_W - 1, GDN_CONV_DIM), F32)],
        scratch_shapes=[pltpu.VMEM((tc + SUBLANES, GDN_CONV_DIM), F32),
                        pltpu.VMEM((GDN_HEADS, GDN_DK, GDN_DV), F32)],
        compiler_params=_params(("parallel", "arbitrary")),
        name="gdn",
    )(gx, gates, gz, buf, conv_w, s0, norm_g)


def _even_out_kernel(x_ref, oft_ref, og_ref, w_ref, g_ref, b_ref, o_ref):
    oft = oft_ref[0]
    for rows in _row_parts(x_ref.shape[1]):
        y = lax.dot_general(oft[:, rows], w_ref[0:FOX_W, :], (((0,), (0,)), ((), ())),
                            preferred_element_type=F32)
        y = y + _dot(og_ref[0, rows, :], w_ref[FOX_W:FOX_W + GDN_VW, :])
        o_ref[0, rows, :] = _layer_norm(ALPHA * x_ref[0, rows, :] + y, g_ref[...], b_ref[...])


def _even_out(x, o_fox_t, o_gdn, w, g, b):
    bsz, t, _ = x.shape
    tm = min(EVEN_ROWS, t)
    row_spec = lambda width: pl.BlockSpec((1, tm, width), lambda bb, i: (bb, i, 0))
    return pl.pallas_call(
        _even_out_kernel,
        grid=(bsz, t // tm),
        in_specs=[row_spec(D_MODEL),
                  pl.BlockSpec((1, FOX_W, tm), lambda bb, i: (bb, 0, i)),
                  row_spec(GDN_VW),
                  _const_spec((FOX_W + GDN_VW, D_MODEL)), _const_spec((1, D_MODEL)), _const_spec((1, D_MODEL))],
        out_specs=row_spec(D_MODEL),
        out_shape=jax.ShapeDtypeStruct((bsz, t, D_MODEL), F32),
        compiler_params=_params(("parallel", "parallel")),
        name="even_out",
    )(x, o_fox_t, o_gdn, w, g, b)


def _odd_kernel(x_ref, h0r_ref, h0i_ref, win_ref, bre_ref, bim_ref, cre_ref, cim_ref, lam_ref, d_ref,
                gw_ref, gb_ref, wout_ref, g_ref, b_ref,
                o_ref, hr_out, hi_out, tre, tim, hr_scr, hi_scr):
    t = pl.program_id(1)
    tm = x_ref.shape[1]
    nblk = D_MODEL // MXU_DIM
    per = S5_LANES // nblk

    @pl.when(t == 0)
    def _():
        hr_scr[...] = h0r_ref[0]
        hi_scr[...] = h0i_ref[0]

    x = x_ref[0]
    u = _dot(x.astype(BF16), win_ref[...])
    ub = u.astype(BF16)
    jper = per // LANES

    def column(j):
        return pl.ds(j, tm, stride=S5_PITCH)

    for kk in range(nblk):
        uk = ub[:, MXU_DIM * kk:MXU_DIM * (kk + 1)]
        b_re = _dot(uk, bre_ref[kk])
        b_im = _dot(uk, bim_ref[kk])
        for jj in range(jper):
            tre[column(jper * kk + jj), :] = b_re[:, LANES * jj:LANES * (jj + 1)]
            tim[column(jper * kk + jj), :] = b_im[:, LANES * jj:LANES * (jj + 1)]

    lr = lam_ref[0]
    li = lam_ref[1]

    def step(i, carry):
        hr, hi = carry
        rows = pl.ds(pl.multiple_of(i * S5_PITCH, SUBLANES), S5_ROWS)
        nr = (lr * hr - li * hi) + tre[rows, :]
        ni = (lr * hi + li * hr) + tim[rows, :]
        tre[rows, :] = nr
        tim[rows, :] = ni
        return nr, ni

    hr, hi = lax.fori_loop(0, tm, step, (hr_scr[...], hi_scr[...]), unroll=SCAN_UNROLL)
    hr_scr[...] = hr
    hi_scr[...] = hi
    hr_out[0] = hr
    hi_out[0] = hi

    ys = []
    for kk in range(nblk):
        s_re = jnp.concatenate([tre[column(jper * kk + jj), :] for jj in range(jper)], axis=1).astype(BF16)
        s_im = jnp.concatenate([tim[column(jper * kk + jj), :] for jj in range(jper)], axis=1).astype(BF16)
        ys.append(_dot(s_re, cre_ref[kk]) - _dot(s_im, cim_ref[kk]))
    y = jnp.concatenate(ys, axis=1) + d_ref[...] * u
    zz = jax.nn.gelu(y)
    gated = zz * jax.nn.sigmoid(_dot(zz.astype(BF16), gw_ref[...]) + gb_ref[...])
    out = _dot(gated.astype(BF16), wout_ref[...])
    o_ref[0] = _layer_norm(ALPHA * x + out, g_ref[...], b_ref[...])


def _odd(x, h0r, h0i, w_in, bre, bim, cre, cim, sw, d, gw, gb, w_out, g, b):
    bsz, t, _ = x.shape
    tm = min(ODD_ROWS, t)
    nblk = D_MODEL // MXU_DIM
    per = S5_LANES // nblk
    st_spec = pl.BlockSpec((1, S5_ROWS, LANES), lambda bb, i: (bb, 0, 0))
    return pl.pallas_call(
        _odd_kernel,
        grid=(bsz, t // tm),
        in_specs=[pl.BlockSpec((1, tm, D_MODEL), lambda bb, i: (bb, i, 0)), st_spec, st_spec,
                  _const_spec((D_MODEL, D_MODEL)),
                  _const_spec((nblk, MXU_DIM, per)), _const_spec((nblk, MXU_DIM, per)),
                  _const_spec((nblk, per, MXU_DIM)), _const_spec((nblk, per, MXU_DIM)),
                  _const_spec((2, S5_ROWS, LANES)), _const_spec((1, D_MODEL)),
                  _const_spec((D_MODEL, D_MODEL)), _const_spec((1, D_MODEL)), _const_spec((D_MODEL, D_MODEL)),
                  _const_spec((1, D_MODEL)), _const_spec((1, D_MODEL))],
        out_specs=[pl.BlockSpec((1, tm, D_MODEL), lambda bb, i: (bb, i, 0)), st_spec, st_spec],
        out_shape=[jax.ShapeDtypeStruct((bsz, t, D_MODEL), F32),
                   jax.ShapeDtypeStruct((bsz, S5_ROWS, LANES), F32),
                   jax.ShapeDtypeStruct((bsz, S5_ROWS, LANES), F32)],
        scratch_shapes=[pltpu.VMEM((tm * S5_PITCH, LANES), F32), pltpu.VMEM((tm * S5_PITCH, LANES), F32),
                        pltpu.VMEM((S5_ROWS, LANES), F32), pltpu.VMEM((S5_ROWS, LANES), F32)],
        compiler_params=_params(("parallel", "arbitrary")),
        name="odd_s5",
    )(x, h0r, h0i, w_in, bre, bim, cre, cim, sw, d, gw, gb, w_out, g, b)


def _prep_even(w_in, b_f, a_log, dt_bias):
    offs = [0]
    for s in EVEN_SPLITS:
        offs.append(offs[-1] + s)
    fq, fk, fv, ff, gx, ga, gb, gz = [w_in[:, offs[i]:offs[i + 1]] for i in range(len(EVEN_SPLITS))]

    pad = jnp.zeros((D_MODEL, LANES - (2 * FOX_HEADS + 2 * GDN_HEADS)), w_in.dtype)
    w = jnp.concatenate([fq, fk, fv, gx, gz, ff, ga, gb, ff, pad], axis=1).astype(BF16)
    zpad = jnp.zeros((LANES - (2 * FOX_HEADS + 2 * GDN_HEADS),), F32)
    bias = jnp.concatenate([b_f.astype(F32), dt_bias.astype(F32), jnp.zeros((GDN_HEADS,), F32), b_f.astype(F32), zpad])
    alog = jnp.concatenate([jnp.zeros((FOX_HEADS,), F32), a_log.astype(F32),
                            jnp.zeros((LANES - FOX_HEADS - GDN_HEADS,), F32)])
    return w, bias.reshape(1, LANES), alog.reshape(1, LANES)


def _block_diag(blocks):
    nblk, g, r, c = blocks.shape
    eye = jnp.eye(g, dtype=blocks.dtype)
    return jnp.einsum('kgrc,gh->kgrhc', blocks, eye).reshape(nblk, g * r, g * c)


def _prep_odd(lam_re, lam_im, log_step, b_re, b_im, c_re, c_im):
    lam = lax.complex(lam_re.astype(F32), lam_im.astype(F32))
    lam_bar = jnp.exp(lam * jnp.exp(log_step.astype(F32))[:, None])
    b_bar = ((lam_bar - 1.0) / lam)[..., None] * lax.complex(b_re.astype(F32), b_im.astype(F32))
    nblk = D_MODEL // MXU_DIM
    gper = S5_GROUPS // nblk
    bt = jnp.swapaxes(b_bar, 1, 2).reshape(nblk, gper, S5_GROUP, S5_STATE)
    bre = _block_diag(jnp.real(bt)).astype(BF16)
    bim = _block_diag(jnp.imag(bt)).astype(BF16)
    ct_re = jnp.swapaxes(c_re.astype(F32), 1, 2).reshape(nblk, gper, S5_STATE, S5_GROUP)
    ct_im = jnp.swapaxes(c_im.astype(F32), 1, 2).reshape(nblk, gper, S5_STATE, S5_GROUP)
    cre = _block_diag(ct_re).astype(BF16)
    cim = _block_diag(ct_im).astype(BF16)
    lam_flat = lam_bar.reshape(S5_ROWS, LANES)
    lam_ri = jnp.stack([jnp.real(lam_flat), jnp.imag(lam_flat)]).astype(F32)
    return bre, bim, cre, cim, lam_ri


def _even_layer(x, fox_past, s0, buf0, w_in, bias, alog, conv_w, norm_g, w_out, g, b):
    bsz, t, _ = x.shape
    q, kx, vt, k, v, gx, gz, gates = _even_in(x, w_in, bias, alog)
    if fox_past is None:
        o_fox_t = _fox_prompt(q, kx, vt)
    else:
        o_fox_t = _fox_sample(q, k, v, gates, *fox_past)
    o_gdn, s_new, new_buf = _gdn(gx, gates, gz, buf0, conv_w, s0, norm_g)
    y = _even_out(x, o_fox_t, o_gdn, w_out, g, b)
    state = (k.reshape(bsz, t, FOX_HEADS, FOX_DH), v.reshape(bsz, t, FOX_HEADS, FOX_DH),
             gates[:, :, :FOX_HEADS], s_new, new_buf)
    return y, state


def kernel(x_prompt, x_sample, cache_fox_k, cache_fox_v, cache_fox_logf, state_gdn, state_gdn_conv, state_s5_re, state_s5_im, ffn_w_in, ffn_w_out, ln_g, ln_b, even_w_in, fox_b_f, gdn_conv_w, gdn_a_log, gdn_dt_bias, gdn_norm_g, even_w_out, odd_w_in, s5_lam_re, s5_lam_im, s5_log_step, s5_b_re, s5_b_im, s5_c_re, s5_c_im, s5_d, s5_glu_w, s5_glu_b, odd_w_out):
    bp, tp, _ = x_prompt.shape
    bs, ts, _ = x_sample.shape
    past = cache_fox_k.shape[2]
    xp, xs = x_prompt, x_sample
    even_p, even_s, odd_p, odd_s = [], [], [], []
    row = lambda a: a.astype(F32).reshape(1, -1)
    ffn_w_in_b = ffn_w_in.astype(BF16)
    ffn_w_out_b = ffn_w_out.astype(BF16)

    def ffn_pair(xp, xs, layer, idx, ln_idx):
        w_in = ffn_w_in_b[layer, idx]
        w_out = ffn_w_out_b[layer, idx]
        g, b = row(ln_g[layer, ln_idx]), row(ln_b[layer, ln_idx])
        xp = _ffn(xp.reshape(bp * tp, D_MODEL), w_in, w_out, g, b).reshape(bp, tp, D_MODEL)
        xs = _ffn(xs.reshape(bs * ts, D_MODEL), w_in, w_out, g, b).reshape(bs, ts, D_MODEL)
        return xp, xs

    for layer in range(DEPTH):
        xp, xs = ffn_pair(xp, xs, layer, 0, 0)
        j = layer // 2
        g1, b1 = row(ln_g[layer, 1]), row(ln_b[layer, 1])
        if layer % 2 == 0:
            w_in, bias, alog = _prep_even(even_w_in[j], fox_b_f[j], gdn_a_log[j], gdn_dt_bias[j])
            common = (w_in, bias, alog, gdn_conv_w[j].astype(F32), row(gdn_norm_g[j]), even_w_out[j].astype(BF16), g1, b1)
            s0 = jnp.zeros((bp, GDN_HEADS, GDN_DK, GDN_DV), F32)
            buf0 = jnp.zeros((bp, CONV_W - 1, GDN_CONV_DIM), F32)
            xp, st_p = _even_layer(xp, None, s0, buf0, *common)
            lc = jnp.pad(cache_fox_logf[j].astype(F32), ((0, 0), (0, 0), (0, LANES - FOX_HEADS)))
            fox_past = (cache_fox_k[j].reshape(bs, past, FOX_W), cache_fox_v[j].reshape(bs, past, FOX_W), lc)
            xs, st_s = _even_layer(xs, fox_past, state_gdn[j].astype(F32), state_gdn_conv[j].astype(F32), *common)
            even_p.append(st_p)
            even_s.append(st_s)
        else:
            bre, bim, cre, cim, sw = _prep_odd(s5_lam_re[j], s5_lam_im[j], s5_log_step[j], s5_b_re[j], s5_b_im[j],
                                               s5_c_re[j], s5_c_im[j])
            common = (odd_w_in[j].astype(BF16), bre, bim, cre, cim, sw, row(s5_d[j]), s5_glu_w[j].astype(BF16),
                      row(s5_glu_b[j]), odd_w_out[j].astype(BF16), g1, b1)
            h0 = jnp.zeros((bp, S5_ROWS, LANES), F32)
            xp, hr, hi = _odd(xp, h0, h0, *common)
            odd_p.append((hr.reshape(bp, S5_GROUPS, S5_STATE), hi.reshape(bp, S5_GROUPS, S5_STATE)))
            xs, hr, hi = _odd(xs, state_s5_re[j].astype(F32).reshape(bs, S5_ROWS, LANES),
                              state_s5_im[j].astype(F32).reshape(bs, S5_ROWS, LANES), *common)
            odd_s.append((hr.reshape(bs, S5_GROUPS, S5_STATE), hi.reshape(bs, S5_GROUPS, S5_STATE)))
        xp, xs = ffn_pair(xp, xs, layer, 1, 2)

    stack = lambda states, idx: jnp.stack([s[idx] for s in states])
    return (xp, xs,
            stack(even_p, 0), stack(even_p, 1), stack(even_p, 2), stack(even_p, 3), stack(even_p, 4),
            stack(odd_p, 0), stack(odd_p, 1),
            stack(even_s, 0), stack(even_s, 1), stack(even_s, 2), stack(even_s, 3), stack(even_s, 4),
            stack(odd_s, 0), stack(odd_s, 1))
```

```python
import functools

import jax
import jax.numpy as jnp
from jax import lax
from jax.experimental import pallas as pl
from jax.experimental.pallas import tpu as pltpu

F32 = jnp.float32
BF16 = jnp.bfloat16

D_MODEL = 1024
DEPTH = 2
CHUNK = 64
FOX_HEADS = 8
FOX_DH = 64
FOX_W = FOX_HEADS * FOX_DH
GDN_HEADS = 4
GDN_DK = 128
GDN_DV = 128
GDN_QK = GDN_HEADS * GDN_DK
GDN_VW = GDN_HEADS * GDN_DV
GDN_CONV_DIM = 2 * GDN_QK + GDN_VW
CONV_W = 4
S5_GROUP = 16
S5_GROUPS = D_MODEL // S5_GROUP
S5_STATE = 64
S5_LANES = S5_GROUPS * S5_STATE
D_FF = ((8 * D_MODEL // 3 + 127) // 128) * 128
ALPHA = (2.0 * DEPTH) ** 0.25
LN_EPS = 1e-5
NORM_EPS = 1e-6
EVEN_SPLITS = (FOX_W, FOX_W, FOX_W, FOX_HEADS, GDN_CONV_DIM, GDN_HEADS, GDN_HEADS, GDN_VW)

LANES = 128
SUBLANES = 8
MXU_DIM = 256
VMEM_LIMIT = 56 * 1024 * 1024

FOX_XW = FOX_HEADS * LANES
_Q0 = 0
_K0 = _Q0 + FOX_W
_V0 = _K0 + FOX_W
_X0 = _V0 + FOX_W
_Z0 = _X0 + GDN_CONV_DIM
_G0 = _Z0 + GDN_VW
EVEN_COLS = _G0 + LANES
_BIAS_LANE = FOX_DH
_ONES_ROW = FOX_DH
_LOGF_LANE, _G_LANE, _BETA_LANE, _CUM_LANE = 0, FOX_HEADS, FOX_HEADS + GDN_HEADS, FOX_HEADS + 2 * GDN_HEADS
NEG_BIG = -1e30
LOG2E = 1.4426950408889634

FFN_ROWS = 1024
FFN_COLS = 256
PART_ROWS = 256
EVEN_ROWS = 512
ATT_ROWS = 1024
ATT_HEADS = 2
GDN_ROWS = 512
GDN_GROUP_ROWS = 512
ODD_ROWS = 512
SCAN_UNROLL = 8
S5_ROWS = S5_LANES // LANES
S5_PITCH = S5_ROWS + SUBLANES


def _dot(a, b):
    return jnp.dot(a, b, preferred_element_type=F32)


def _dot_nt(a, b):
    return lax.dot_general(a, b, (((1,), (1,)), ((), ())), preferred_element_type=F32)


def _layer_norm(z, g, b):
    mu = jnp.mean(z, -1, keepdims=True)
    d = z - mu
    var = jnp.mean(d * d, -1, keepdims=True)
    return d * lax.rsqrt(var + LN_EPS) * g + b


def _split3(x):
    hi = x.astype(BF16)
    r = x - hi.astype(F32)
    mid = r.astype(BF16)
    lo = (r - mid.astype(F32)).astype(BF16)
    return hi, mid, lo


def _tri_cumsum(tri, x):
    hi, mid, lo = _split3(x)
    return _dot(tri, hi) + _dot(tri, mid) + _dot(tri, lo)


def _row_parts(rows):
    n = max(1, rows // PART_ROWS)
    return [slice(r * (rows // n), (r + 1) * (rows // n)) for r in range(n)]


def _const_spec(shape):
    nd = len(shape)
    return pl.BlockSpec(shape, lambda *_: (0,) * nd, pipeline_mode=pl.Buffered(1))


def _params(sem):
    return pltpu.CompilerParams(dimension_semantics=sem, vmem_limit_bytes=VMEM_LIMIT)


def _ffn_kernel(x_ref, win_ref, wout_ref, g_ref, b_ref, o_ref, h_ref):
    halves = _row_parts(x_ref.shape[0])
    for rows in halves:
        xb = x_ref[rows, :].astype(BF16)
        for c in range(D_FF // FFN_COLS):
            lo = c * FFN_COLS
            a = _dot(xb, win_ref[:, lo:lo + FFN_COLS])
            b = _dot(xb, win_ref[:, D_FF + lo:D_FF + lo + FFN_COLS])
            h_ref[rows, lo:lo + FFN_COLS] = (jax.nn.silu(a) * b).astype(BF16)
    for rows in halves:
        y = _dot(h_ref[rows, :], wout_ref[...])
        o_ref[rows, :] = _layer_norm(ALPHA * x_ref[rows, :] + 0.5 * y, g_ref[...], b_ref[...])


def _ffn(x, w_in, w_out, g, b):
    n = x.shape[0]
    tm = min(FFN_ROWS, n)
    return pl.pallas_call(
        _ffn_kernel,
        grid=(n // tm,),
        in_specs=[pl.BlockSpec((tm, D_MODEL), lambda i: (i, 0)),
                  _const_spec((D_MODEL, 2 * D_FF)),
                  _const_spec((D_FF, D_MODEL)),
                  _const_spec((1, D_MODEL)),
                  _const_spec((1, D_MODEL))],
        out_specs=pl.BlockSpec((tm, D_MODEL), lambda i: (i, 0)),
        out_shape=jax.ShapeDtypeStruct((n, D_MODEL), F32),
        scratch_shapes=[pltpu.VMEM((tm, D_FF), BF16)],
        compiler_params=_params(("parallel",)),
        name="ffn_ln",
    )(x, w_in, w_out, g, b)


def _even_in_kernel(x_ref, w_ref, bias_ref, alog_ref,
                    q_ref, kx_ref, vt_ref, k_ref, v_ref, gx_ref, gz_ref, gates_ref,
                    carry_ref):
    t = pl.program_id(1)
    tm = x_ref.shape[1]

    @pl.when(t == 0)
    def _():
        carry_ref[...] = jnp.zeros_like(carry_ref)

    for rows in _row_parts(tm):
        n = rows.stop - rows.start
        xb = x_ref[0, rows, :].astype(BF16)

        def mm(lo, width):
            return _dot(xb, w_ref[:, lo:lo + width])

        lane = lax.broadcasted_iota(jnp.int32, (n, LANES), 1)
        raw = mm(_G0, LANES) + bias_ref[...]
        logf = jax.nn.log_sigmoid(raw)
        gdec = -jnp.exp(alog_ref[...]) * jax.nn.softplus(raw)
        beta = jax.nn.sigmoid(raw)
        gates = jnp.where(lane < _G_LANE, logf, jnp.where(lane < _BETA_LANE, gdec, jnp.where(lane < _CUM_LANE, beta, logf)))
        row = lax.broadcasted_iota(jnp.int32, (n, n), 0)
        col = lax.broadcasted_iota(jnp.int32, (n, n), 1)
        tri = jnp.where(row >= col, 1.0, 0.0).astype(BF16)
        cum = _tri_cumsum(tri, logf) + carry_ref[...]
        carry_ref[...] = cum[n - 1:n, :]
        gates_ref[0, rows, :] = jnp.where((lane >= _CUM_LANE) & (lane < _CUM_LANE + FOX_HEADS), cum, gates)

        k = mm(_K0, FOX_W)
        k_ref[0, rows, :] = k
        v = mm(_V0, FOX_W)
        v_ref[0, rows, :] = v
        gx_ref[0, rows, :] = mm(_X0, GDN_CONV_DIM)
        gz_ref[0, rows, :] = mm(_Z0, GDN_VW)

        c_parts = [part.astype(F32) for part in _split3(cum * LOG2E)]
        q = mm(_Q0, FOX_W) * (FOX_DH ** -0.5 * LOG2E)
        for h in range(FOX_HEADS):
            pair = slice(LANES * (h // 2), LANES * (h // 2 + 1))
            q_h, k_h = q[:, pair], k[:, pair]
            if h % 2:
                q_h, k_h = pltpu.roll(q_h, FOX_DH, 1), pltpu.roll(k_h, FOX_DH, 1)
            ch = [part[:, _CUM_LANE + h:_CUM_LANE + h + 1] for part in c_parts]
            q_ext = jnp.where(lane < _BIAS_LANE + 3, 0.0, jnp.where(lane < _BIAS_LANE + 6, 1.0, 0.0))
            k_ext = jnp.where(lane < _BIAS_LANE + 3, 1.0, 0.0)
            for i in range(3):
                q_ext = jnp.where(lane == _BIAS_LANE + i, ch[i], q_ext)
                k_ext = jnp.where(lane == _BIAS_LANE + 3 + i, -ch[i], k_ext)
            hs = slice(LANES * h, LANES * (h + 1))
            q_ref[0, rows, hs] = jnp.where(lane < FOX_DH, q_h, q_ext).astype(BF16)
            kx_ref[0, rows, hs] = jnp.where(lane < FOX_DH, k_h, k_ext).astype(BF16)

        v_t = v.T
        pack = 2 * SUBLANES
        ones_blk = jnp.where(lax.broadcasted_iota(jnp.int32, (pack, n), 0) == 0, 1.0, 0.0).astype(BF16)
        for h in range(FOX_HEADS):
            base = LANES * h
            vt_ref[0, base:base + FOX_DH, rows] = v_t[FOX_DH * h:FOX_DH * (h + 1), :].astype(BF16)
            vt_ref[0, base + _ONES_ROW:base + _ONES_ROW + pack, rows] = ones_blk
            vt_ref[0, base + _ONES_ROW + pack:base + LANES, rows] = jnp.zeros((LANES - _ONES_ROW - pack, n), BF16)


def _even_in(x, w, bias, alog):
    bsz, t, _ = x.shape
    tm = min(EVEN_ROWS, t)
    row_spec = lambda width: pl.BlockSpec((1, tm, width), lambda b, i: (b, i, 0))
    shp = lambda width, dt: jax.ShapeDtypeStruct((bsz, t, width), dt)
    return pl.pallas_call(
        _even_in_kernel,
        grid=(bsz, t // tm),
        in_specs=[row_spec(D_MODEL), _const_spec((D_MODEL, EVEN_COLS)), _const_spec((1, LANES)), _const_spec((1, LANES))],
        out_specs=[row_spec(FOX_XW), row_spec(FOX_XW),
                   pl.BlockSpec((1, FOX_XW, tm), lambda b, i: (b, 0, i)),
                   row_spec(FOX_W), row_spec(FOX_W), row_spec(GDN_CONV_DIM), row_spec(GDN_VW), row_spec(LANES)],
        out_shape=[shp(FOX_XW, BF16), shp(FOX_XW, BF16), jax.ShapeDtypeStruct((bsz, FOX_XW, t), BF16),
                   shp(FOX_W, F32), shp(FOX_W, F32), shp(GDN_CONV_DIM, F32), shp(GDN_VW, F32), shp(LANES, F32)],
        scratch_shapes=[pltpu.VMEM((1, LANES), F32)],
        compiler_params=_params(("parallel", "arbitrary")),
        name="even_in",
    )(x, w, bias, alog)


def _fox_prompt_kernel(q_ref, k_ref, vt_ref, o_ref, m_ref, acc_ref, s0_ref, s1_ref, p0_ref, p1_ref, x0_ref, x1_ref):
    qi = pl.program_id(2)
    tq = q_ref.shape[1]
    tk = s0_ref.shape[1]
    nh = q_ref.shape[2] // LANES
    heads = [slice(LANES * a, LANES * (a + 1)) for a in range(nh)]
    m_ref[...] = jnp.full(m_ref.shape, NEG_BIG, F32)
    acc_ref[...] = jnp.zeros_like(acc_ref)
    key_pos = lax.broadcasted_iota(jnp.int32, (tk, tq), 0)
    qry_pos = lax.broadcasted_iota(jnp.int32, (tk, tq), 1)

    upper = slice(tk, tq)
    visible_upper = (lax.broadcasted_iota(jnp.int32, (tk, tq - tk), 0)
                     <= lax.broadcasted_iota(jnp.int32, (tk, tq - tk), 1))
    even = (s0_ref, x0_ref, p0_ref)
    odd = (s1_ref, x1_ref, p1_ref)

    def scores(j, dst, key_shift):
        off = pl.multiple_of(j * tk, tk)
        for a in range(nh):
            if key_shift == tk:
                s_t = _dot_nt(k_ref[0, pl.ds(off, tk), heads[a]], q_ref[0, tk:tq, heads[a]])
                s_t = jnp.where(visible_upper, s_t, NEG_BIG)
                dst[0][a, :, upper] = s_t
                dst[1][a, :, upper] = jnp.broadcast_to(jnp.max(s_t, axis=0, keepdims=True), (SUBLANES, tq - tk))
                continue
            s_t = _dot_nt(k_ref[0, pl.ds(off, tk), heads[a]], q_ref[0, :, heads[a]])
            if key_shift is not None:
                s_t = jnp.where(key_pos + key_shift <= qry_pos, s_t, NEG_BIG)
            dst[0][a] = s_t
            dst[1][a] = jnp.broadcast_to(jnp.max(s_t, axis=0, keepdims=True), (SUBLANES, tq))

    def weighted_values(j, p_src, a):
        off = pl.multiple_of(j * tk, tk)
        return _dot(vt_ref[0, heads[a], pl.ds(off, tk)], p_src[a])

    def step(j, cur, prev, prefetch, next_shift=None):
        if prefetch:
            scores(j + 1, prev, next_shift)
        j_prev = jnp.maximum(j - 1, 0)
        for a in range(nh):
            m_prev = m_ref[a]
            m_new = jnp.maximum(m_prev, cur[1][a])
            r = weighted_values(j_prev, prev[2], a)
            cur[2][a] = jnp.exp2(cur[0][a] - m_new[0:1]).astype(BF16)
            acc_ref[a] = jnp.exp2(m_prev[0:1] - m_new[0:1]) * (acc_ref[a] + r)
            m_ref[a] = m_new

    p1_ref[...] = jnp.zeros_like(p1_ref)

    @pl.when(qi == 0)
    def _():
        scores(0, even, 0)

    @pl.when(qi > 0)
    def _():
        scores(0, even, None)

        def two_tiles(i, c):
            j = 2 * i
            step(j, even, odd, True)
            step(j + 1, odd, even, True)
            return c

        lax.fori_loop(0, qi - 1, two_tiles, 0)
        step(2 * qi - 2, even, odd, True)
        step(2 * qi - 1, odd, even, True, 0)

    step(2 * qi, even, odd, True, tk)

    j_last = 2 * qi + 1
    off_last = pl.multiple_of(j_last * tk, tk)
    for a in range(nh):
        out_rows = slice(FOX_DH * a, FOX_DH * (a + 1))
        r = weighted_values(j_last - 1, p0_ref, a)
        m_prev = m_ref[a, :, upper]
        m_new = jnp.maximum(m_prev, x1_ref[a, :, upper])
        p_t = jnp.exp2(s1_ref[a, :, upper] - m_new[0:1]).astype(BF16)
        lo = acc_ref[a, :, 0:tk] + r[:, 0:tk]
        hi = jnp.exp2(m_prev[0:1] - m_new[0:1]) * (acc_ref[a, :, upper] + r[:, upper])
        hi = hi + _dot(vt_ref[0, heads[a], pl.ds(off_last, tk)], p_t)
        o_ref[0, out_rows, 0:tk] = (lo[0:FOX_DH] / lo[_ONES_ROW:_ONES_ROW + 1]).astype(BF16)
        o_ref[0, out_rows, upper] = (hi[0:FOX_DH] / hi[_ONES_ROW:_ONES_ROW + 1]).astype(BF16)


def _fox_prompt(q, kx, vt):
    bsz, t, _ = q.shape
    tq = min(ATT_ROWS, t)
    nh = ATT_HEADS
    return pl.pallas_call(
        _fox_prompt_kernel,
        grid=(bsz, FOX_HEADS // nh, t // tq),
        in_specs=[pl.BlockSpec((1, tq, nh * LANES), lambda b, p, i: (b, i, p)),
                  pl.BlockSpec((1, t, nh * LANES), lambda b, p, i: (b, 0, p)),
                  pl.BlockSpec((1, nh * LANES, t), lambda b, p, i: (b, p, 0))],
        out_specs=pl.BlockSpec((1, nh * FOX_DH, tq), lambda b, p, i: (b, p, i)),
        out_shape=jax.ShapeDtypeStruct((bsz, FOX_W, t), BF16),
        scratch_shapes=[pltpu.VMEM((nh, SUBLANES, tq), F32), pltpu.VMEM((nh, LANES, tq), F32),
                        pltpu.VMEM((nh, tq // 2, tq), F32), pltpu.VMEM((nh, tq // 2, tq), F32),
                        pltpu.VMEM((nh, tq // 2, tq), BF16), pltpu.VMEM((nh, tq // 2, tq), BF16),
                        pltpu.VMEM((nh, SUBLANES, tq), F32), pltpu.VMEM((nh, SUBLANES, tq), F32)],
        compiler_params=_params(("parallel", "parallel", "arbitrary")),
        name="fox_prompt",
    )(q, kx, vt)


def _fox_sample_kernel(q_ref, kn_ref, vn_ref, gates_ref, kc_ref, vc_ref, lc_ref, o_ref):
    past = kc_ref.shape[1]
    t = q_ref.shape[1]
    row = lax.broadcasted_iota(jnp.int32, (past, past), 0)
    col = lax.broadcasted_iota(jnp.int32, (past, past), 1)
    tri = jnp.where(row >= col, 1.0, 0.0).astype(BF16)
    cum = _tri_cumsum(tri, lc_ref[0])
    ck_cache = (cum - cum[past - 1:past, :]).T
    gts = gates_ref[0]
    ck_new = gts.T
    kc = kc_ref[0].astype(BF16)
    vc = vc_ref[0].astype(BF16)
    kn = kn_ref[0].astype(BF16)
    vn = vn_ref[0].astype(BF16)
    causal = lax.broadcasted_iota(jnp.int32, (t, t), 1) <= lax.broadcasted_iota(jnp.int32, (t, t), 0)
    lane = lax.broadcasted_iota(jnp.int32, (t, LANES), 1)
    outs = []
    for h in range(FOX_HEADS):
        ls = slice(LANES * (h // 2), LANES * (h // 2 + 1))
        qh = q_ref[0, :, LANES * h:LANES * (h + 1)].astype(F32)
        if h % 2 == 0:
            qh = jnp.where(lane < FOX_DH, qh, 0.0).astype(BF16)
        else:
            qh = jnp.where(lane >= FOX_DH, pltpu.roll(qh, FOX_DH, 1), 0.0).astype(BF16)
        cq = gts[:, _CUM_LANE + h:_CUM_LANE + h + 1]
        cq = cq * LOG2E
        s1 = (_dot_nt(qh, kc[:, ls]) + cq) - ck_cache[h:h + 1, :] * LOG2E
        s2 = (_dot_nt(qh, kn[:, ls]) + cq) - ck_new[_CUM_LANE + h:_CUM_LANE + h + 1, :] * LOG2E
        s2 = jnp.where(causal, s2, NEG_BIG)
        m = jnp.maximum(jnp.max(s1, axis=1, keepdims=True), jnp.max(s2, axis=1, keepdims=True))
        p1 = jnp.exp2(s1 - m)
        p2 = jnp.exp2(s2 - m)
        l = jnp.sum(p1, axis=1, keepdims=True) + jnp.sum(p2, axis=1, keepdims=True)
        outs.append((_dot(p1.astype(BF16), vc[:, ls]) + _dot(p2.astype(BF16), vn[:, ls])) / l)
    o = jnp.concatenate([jnp.where(lane < FOX_DH, outs[2 * p], outs[2 * p + 1]) for p in range(FOX_HEADS // 2)], axis=1)
    o_ref[0] = o.T.astype(BF16)


def _fox_sample(q, k, v, gates, past_k, past_v, past_logf):
    bsz, t, _ = k.shape
    past = past_k.shape[1]
    blk = lambda rows, width: pl.BlockSpec((1, rows, width), lambda b: (b, 0, 0))
    return pl.pallas_call(
        _fox_sample_kernel,
        grid=(bsz,),
        in_specs=[blk(t, FOX_XW), blk(t, FOX_W), blk(t, FOX_W), blk(t, LANES),
                  blk(past, FOX_W), blk(past, FOX_W), blk(past, LANES)],
        out_specs=blk(FOX_W, t),
        out_shape=jax.ShapeDtypeStruct((bsz, FOX_W, t), BF16),
        compiler_params=_params(("parallel",)),
        name="fox_sample",
    )(q, k, v, gates, past_k, past_v, past_logf)


def _gdn_kernel(x_ref, gates_ref, z_ref, buf_ref, cw_ref, s0_ref, ng_ref, res_ref, oft_ref, wout_ref, lg_ref, lb_ref,
                y_ref, sfin_ref, cbuf_ref, xbuf, s_scr, o_ref, yf_ref, *, chunk):
    t = pl.program_id(1)
    tc = x_ref.shape[1]
    pad = SUBLANES
    yf_ref[...] = lax.dot_general(oft_ref[0], wout_ref[0:FOX_W, :], (((0,), (0,)), ((), ())),
                                  preferred_element_type=F32)

    @pl.when(t == 0)
    def _():
        xbuf[0:pad, :] = jnp.zeros((pad, GDN_CONV_DIM), F32)
        xbuf[pad - (CONV_W - 1):pad, :] = buf_ref[0]
        s_scr[...] = s0_ref[0]

    @pl.when(t > 0)
    def _():
        xbuf[0:pad, :] = xbuf[tc:tc + pad, :]

    xbuf[pad:pad + tc, :] = x_ref[0]
    cw = cw_ref[...]
    cbuf_ref[0] = xbuf[pad + tc - (CONV_W - 1):pad + tc, :]

    grp = min(tc, GDN_GROUP_ROWS)
    shift = chunk.bit_length() - 1
    row = lax.broadcasted_iota(jnp.int32, (grp, grp), 0)
    col = lax.broadcasted_iota(jnp.int32, (grp, grp), 1)
    tri = jnp.where((row >= col) & ((row >> shift) == (col >> shift)), 1.0, 0.0).astype(BF16)
    r_c = lax.broadcasted_iota(jnp.int32, (chunk, chunk), 0)
    c_c = lax.broadcasted_iota(jnp.int32, (chunk, chunk), 1)
    lower = r_c >= c_c
    strict = r_c > c_c
    eye = jnp.where(r_c == c_c, 1.0, 0.0)
    ng = ng_ref[...]

    heads = range(GDN_HEADS)
    chunks = range(grp // chunk)
    rows = [slice(c * chunk, (c + 1) * chunk) for c in chunks]
    cols = [slice(GDN_DK * h, GDN_DK * (h + 1)) for h in heads]

    def prepare(base):
        conv = cw[0:1] * xbuf[pad - 3 + base:pad - 3 + base + grp, :]
        for j in range(1, CONV_W):
            conv = conv + cw[j:j + 1] * xbuf[pad - 3 + j + base:pad - 3 + j + base + grp, :]
        act = jax.nn.silu(conv)
        gts = gates_ref[0, base:base + grp, :]
        gcum = _tri_cumsum(tri, gts)
        gcum_t = gcum.T

        kn, kn_b, kbeta_b, qn_b, qg, rhs, gc = [], [], [], [], [], [], []
        for h in heads:
            q = act[:, cols[h]]
            k = act[:, GDN_QK + GDN_DK * h:GDN_QK + GDN_DK * (h + 1)]
            v = act[:, 2 * GDN_QK + GDN_DV * h:2 * GDN_QK + GDN_DV * (h + 1)]
            qn = q * lax.rsqrt(jnp.sum(q * q, -1, keepdims=True) + NORM_EPS) * (GDN_DK ** -0.5)
            kn.append(k * lax.rsqrt(jnp.sum(k * k, -1, keepdims=True) + NORM_EPS))
            beta = gts[:, _BETA_LANE + h:_BETA_LANE + h + 1]
            gc.append(gcum[:, _G_LANE + h:_G_LANE + h + 1])
            eg = jnp.exp(gc[h])
            kbeta = kn[h] * beta
            rhs.append(jnp.concatenate([v * beta, kbeta * eg], axis=1).astype(BF16))
            qg.append((qn * eg).astype(BF16))
            kn_b.append(kn[h].astype(BF16))
            kbeta_b.append(kbeta.astype(BF16))
            qn_b.append(qn.astype(BF16))

        units = [(c, h) for c in chunks for h in heads]
        decay, attn, inv, power = {}, {}, {}, {}
        for c, h in units:
            gcc = gc[h][rows[c]]
            gcr = gcum_t[_G_LANE + h:_G_LANE + h + 1, rows[c]]
            decay[c, h] = jnp.where(lower, jnp.exp(jnp.where(lower, gcc - gcr, 0.0)), 0.0)
        for c, h in units:
            power[c, h] = jnp.where(strict, _dot_nt(kbeta_b[h][rows[c]], kn_b[h][rows[c]]) * decay[c, h], 0.0)
            inv[c, h] = eye - power[c, h]
        p = 1
        while 2 * p < chunk:
            for u in units:
                pb = power[u].astype(BF16)
                power[u] = _dot(pb, pb)
            for u in units:
                inv[u] = inv[u] + _dot(inv[u].astype(BF16), power[u].astype(BF16))
            p *= 2
        uw = {}
        for c, h in units:
            uw[c, h] = _dot(inv[c, h].astype(BF16), rhs[h][rows[c]])
        for c, h in units:
            attn[c, h] = (_dot_nt(qn_b[h][rows[c]], kn_b[h][rows[c]]) * decay[c, h]).astype(BF16)
        return base, uw, attn, qg, kn, gc

    def recur(prepared):
        base, uw, attn, qg, kn, gc = prepared
        for c in chunks:
            out_rows = slice(base + c * chunk, base + (c + 1) * chunk)
            state = [s_scr[h] for h in heads]
            g_last = [gc[h][rows[c]][chunk - 1:chunk] for h in heads]
            ws = [_dot(jnp.concatenate([uw[c, h][:, GDN_DV:].astype(BF16), qg[h][rows[c]]], axis=0),
                       state[h].astype(BF16)) for h in heads]
            v_new = [uw[c, h][:, :GDN_DV] - ws[h][:chunk] for h in heads]
            v_new_b = [v_new[h].astype(BF16) for h in heads]
            kdec_t = [(kn[h][rows[c]] * jnp.exp(g_last[h] - gc[h][rows[c]])).T.astype(BF16) for h in heads]
            for h in heads:
                s_scr[h] = state[h] * jnp.exp(g_last[h]) + _dot(kdec_t[h], v_new_b[h])
            for h in heads:
                o = ws[h][chunk:] + _dot(attn[c, h], v_new_b[h])
                z = z_ref[0, out_rows, cols[h]]
                o = o * lax.rsqrt(jnp.mean(o * o, -1, keepdims=True) + NORM_EPS) * ng * jax.nn.silu(z)
                o_ref[0, out_rows, cols[h]] = o.astype(BF16)

    pending = prepare(0)
    for g in range(1, tc // grp):
        recur(pending)
        pending = prepare(g * grp)
    recur(pending)

    for part in _row_parts(tc):
        y = yf_ref[part, :] + _dot(o_ref[0, part, :], wout_ref[FOX_W:FOX_W + GDN_VW, :])
        y_ref[0, part, :] = _layer_norm(ALPHA * res_ref[0, part, :] + y, lg_ref[...], lb_ref[...])

    sfin_ref[0] = s_scr[...]


def _gdn(gx, gates, gz, buf, conv_w, s0, norm_g, x, o_fox_t, w_out, g, b):
    bsz, t, _ = gx.shape
    chunk = min(t, CHUNK)
    tc = min(GDN_ROWS, t)
    row_spec = lambda width: pl.BlockSpec((1, tc, width), lambda b, i: (b, i, 0))
    return pl.pallas_call(
        functools.partial(_gdn_kernel, chunk=chunk),
        grid=(bsz, t // tc),
        in_specs=[row_spec(GDN_CONV_DIM), row_spec(LANES), row_spec(GDN_VW),
                  pl.BlockSpec((1, CONV_W - 1, GDN_CONV_DIM), lambda b, i: (b, 0, 0)),
                  _const_spec((CONV_W, GDN_CONV_DIM)),
                  pl.BlockSpec((1, GDN_HEADS, GDN_DK, GDN_DV), lambda b, i: (b, 0, 0, 0)),
                  _const_spec((1, GDN_DV)),
                  row_spec(D_MODEL),
                  pl.BlockSpec((1, FOX_W, tc), lambda b, i: (b, 0, i)),
                  _const_spec((FOX_W + GDN_VW, D_MODEL)), _const_spec((1, D_MODEL)), _const_spec((1, D_MODEL))],
        out_specs=[row_spec(D_MODEL),
                   pl.BlockSpec((1, GDN_HEADS, GDN_DK, GDN_DV), lambda b, i: (b, 0, 0, 0)),
                   pl.BlockSpec((1, CONV_W - 1, GDN_CONV_DIM), lambda b, i: (b, 0, 0))],
        out_shape=[jax.ShapeDtypeStruct((bsz, t, D_MODEL), F32),
                   jax.ShapeDtypeStruct((bsz, GDN_HEADS, GDN_DK, GDN_DV), F32),
                   jax.ShapeDtypeStruct((bsz, CONV_W - 1, GDN_CONV_DIM), F32)],
        scratch_shapes=[pltpu.VMEM((tc + SUBLANES, GDN_CONV_DIM), F32),
                        pltpu.VMEM((GDN_HEADS, GDN_DK, GDN_DV), F32),
                        pltpu.VMEM((1, tc, GDN_VW), BF16),
                        pltpu.VMEM((tc, D_MODEL), F32)],
        compiler_params=_params(("parallel", "arbitrary")),
        name="gdn_out",
    )(gx, gates, gz, buf, conv_w, s0, norm_g, x, o_fox_t, w_out, g, b)


def _odd_kernel(x_ref, h0r_ref, h0i_ref, win_ref, bre_ref, bim_ref, cre_ref, cim_ref, lam_ref, d_ref,
                gw_ref, gb_ref, wout_ref, g_ref, b_ref,
                o_ref, hr_out, hi_out, tre, tim, hr_scr, hi_scr):
    t = pl.program_id(1)
    tm = x_ref.shape[1]
    nblk = D_MODEL // MXU_DIM
    per = S5_LANES // nblk

    @pl.when(t == 0)
    def _():
        hr_scr[...] = h0r_ref[0]
        hi_scr[...] = h0i_ref[0]

    x = x_ref[0]
    u = _dot(x.astype(BF16), win_ref[...])
    ub = u.astype(BF16)
    jper = per // LANES

    def column(j):
        return pl.ds(j, tm, stride=S5_PITCH)

    for kk in range(nblk):
        uk = ub[:, MXU_DIM * kk:MXU_DIM * (kk + 1)]
        b_re = _dot(uk, bre_ref[kk])
        b_im = _dot(uk, bim_ref[kk])
        for jj in range(jper):
            tre[column(jper * kk + jj), :] = b_re[:, LANES * jj:LANES * (jj + 1)]
            tim[column(jper * kk + jj), :] = b_im[:, LANES * jj:LANES * (jj + 1)]

    lr = lam_ref[0]
    li = lam_ref[1]

    def step(i, carry):
        hr, hi = carry
        rows = pl.ds(pl.multiple_of(i * S5_PITCH, SUBLANES), S5_ROWS)
        nr = (lr * hr - li * hi) + tre[rows, :]
        ni = (lr * hi + li * hr) + tim[rows, :]
        tre[rows, :] = nr
        tim[rows, :] = ni
        return nr, ni

    hr, hi = lax.fori_loop(0, tm, step, (hr_scr[...], hi_scr[...]), unroll=SCAN_UNROLL)
    hr_scr[...] = hr
    hi_scr[...] = hi
    hr_out[0] = hr
    hi_out[0] = hi

    ys = []
    for kk in range(nblk):
        s_re = jnp.concatenate([tre[column(jper * kk + jj), :] for jj in range(jper)], axis=1).astype(BF16)
        s_im = jnp.concatenate([tim[column(jper * kk + jj), :] for jj in range(jper)], axis=1).astype(BF16)
        ys.append(_dot(s_re, cre_ref[kk]) - _dot(s_im, cim_ref[kk]))
    y = jnp.concatenate(ys, axis=1) + d_ref[...] * u
    zz = jax.nn.gelu(y)
    gated = zz * jax.nn.sigmoid(_dot(zz.astype(BF16), gw_ref[...]) + gb_ref[...])
    out = _dot(gated.astype(BF16), wout_ref[...])
    o_ref[0] = _layer_norm(ALPHA * x + out, g_ref[...], b_ref[...])


def _odd(x, h0r, h0i, w_in, bre, bim, cre, cim, sw, d, gw, gb, w_out, g, b):
    bsz, t, _ = x.shape
    tm = min(ODD_ROWS, t)
    nblk = D_MODEL // MXU_DIM
    per = S5_LANES // nblk
    st_spec = pl.BlockSpec((1, S5_ROWS, LANES), lambda bb, i: (bb, 0, 0))
    return pl.pallas_call(
        _odd_kernel,
        grid=(bsz, t // tm),
        in_specs=[pl.BlockSpec((1, tm, D_MODEL), lambda bb, i: (bb, i, 0)), st_spec, st_spec,
                  _const_spec((D_MODEL, D_MODEL)),
                  _const_spec((nblk, MXU_DIM, per)), _const_spec((nblk, MXU_DIM, per)),
                  _const_spec((nblk, per, MXU_DIM)), _const_spec((nblk, per, MXU_DIM)),
                  _const_spec((2, S5_ROWS, LANES)), _const_spec((1, D_MODEL)),
                  _const_spec((D_MODEL, D_MODEL)), _const_spec((1, D_MODEL)), _const_spec((D_MODEL, D_MODEL)),
                  _const_spec((1, D_MODEL)), _const_spec((1, D_MODEL))],
        out_specs=[pl.BlockSpec((1, tm, D_MODEL), lambda bb, i: (bb, i, 0)), st_spec, st_spec],
        out_shape=[jax.ShapeDtypeStruct((bsz, t, D_MODEL), F32),
                   jax.ShapeDtypeStruct((bsz, S5_ROWS, LANES), F32),
                   jax.ShapeDtypeStruct((bsz, S5_ROWS, LANES), F32)],
        scratch_shapes=[pltpu.VMEM((tm * S5_PITCH, LANES), F32), pltpu.VMEM((tm * S5_PITCH, LANES), F32),
                        pltpu.VMEM((S5_ROWS, LANES), F32), pltpu.VMEM((S5_ROWS, LANES), F32)],
        compiler_params=_params(("parallel", "arbitrary")),
        name="odd_s5",
    )(x, h0r, h0i, w_in, bre, bim, cre, cim, sw, d, gw, gb, w_out, g, b)


def _prep_even(w_in, b_f, a_log, dt_bias):
    offs = [0]
    for s in EVEN_SPLITS:
        offs.append(offs[-1] + s)
    fq, fk, fv, ff, gx, ga, gb, gz = [w_in[:, offs[i]:offs[i + 1]] for i in range(len(EVEN_SPLITS))]

    pad = jnp.zeros((D_MODEL, LANES - (2 * FOX_HEADS + 2 * GDN_HEADS)), w_in.dtype)
    w = jnp.concatenate([fq, fk, fv, gx, gz, ff, ga, gb, ff, pad], axis=1).astype(BF16)
    zpad = jnp.zeros((LANES - (2 * FOX_HEADS + 2 * GDN_HEADS),), F32)
    bias = jnp.concatenate([b_f.astype(F32), dt_bias.astype(F32), jnp.zeros((GDN_HEADS,), F32), b_f.astype(F32), zpad])
    alog = jnp.concatenate([jnp.zeros((FOX_HEADS,), F32), a_log.astype(F32),
                            jnp.zeros((LANES - FOX_HEADS - GDN_HEADS,), F32)])
    return w, bias.reshape(1, LANES), alog.reshape(1, LANES)


def _block_diag(blocks):
    nblk, g, r, c = blocks.shape
    eye = jnp.eye(g, dtype=blocks.dtype)
    return jnp.einsum('kgrc,gh->kgrhc', blocks, eye).reshape(nblk, g * r, g * c)


def _prep_odd(lam_re, lam_im, log_step, b_re, b_im, c_re, c_im):
    lam = lax.complex(lam_re.astype(F32), lam_im.astype(F32))
    lam_bar = jnp.exp(lam * jnp.exp(log_step.astype(F32))[:, None])
    b_bar = ((lam_bar - 1.0) / lam)[..., None] * lax.complex(b_re.astype(F32), b_im.astype(F32))
    nblk = D_MODEL // MXU_DIM
    gper = S5_GROUPS // nblk
    bt = jnp.swapaxes(b_bar, 1, 2).reshape(nblk, gper, S5_GROUP, S5_STATE)
    bre = _block_diag(jnp.real(bt)).astype(BF16)
    bim = _block_diag(jnp.imag(bt)).astype(BF16)
    ct_re = jnp.swapaxes(c_re.astype(F32), 1, 2).reshape(nblk, gper, S5_STATE, S5_GROUP)
    ct_im = jnp.swapaxes(c_im.astype(F32), 1, 2).reshape(nblk, gper, S5_STATE, S5_GROUP)
    cre = _block_diag(ct_re).astype(BF16)
    cim = _block_diag(ct_im).astype(BF16)
    lam_flat = lam_bar.reshape(S5_ROWS, LANES)
    lam_ri = jnp.stack([jnp.real(lam_flat), jnp.imag(lam_flat)]).astype(F32)
    return bre, bim, cre, cim, lam_ri


def _even_layer(x, fox_past, s0, buf0, w_in, bias, alog, conv_w, norm_g, w_out, g, b):
    bsz, t, _ = x.shape
    q, kx, vt, k, v, gx, gz, gates = _even_in(x, w_in, bias, alog)
    if fox_past is None:
        o_fox_t = _fox_prompt(q, kx, vt)
    else:
        o_fox_t = _fox_sample(q, k, v, gates, *fox_past)
    y, s_new, new_buf = _gdn(gx, gates, gz, buf0, conv_w, s0, norm_g, x, o_fox_t, w_out, g, b)
    state = (k.reshape(bsz, t, FOX_HEADS, FOX_DH), v.reshape(bsz, t, FOX_HEADS, FOX_DH),
             gates[:, :, :FOX_HEADS], s_new, new_buf)
    return y, state


def kernel(x_prompt, x_sample, cache_fox_k, cache_fox_v, cache_fox_logf, state_gdn, state_gdn_conv, state_s5_re, state_s5_im, ffn_w_in, ffn_w_out, ln_g, ln_b, even_w_in, fox_b_f, gdn_conv_w, gdn_a_log, gdn_dt_bias, gdn_norm_g, even_w_out, odd_w_in, s5_lam_re, s5_lam_im, s5_log_step, s5_b_re, s5_b_im, s5_c_re, s5_c_im, s5_d, s5_glu_w, s5_glu_b, odd_w_out):
    bp, tp, _ = x_prompt.shape
    bs, ts, _ = x_sample.shape
    past = cache_fox_k.shape[2]
    xp, xs = x_prompt, x_sample
    even_p, even_s, odd_p, odd_s = [], [], [], []
    row = lambda a: a.astype(F32).reshape(1, -1)
    ffn_w_in_b = ffn_w_in.astype(BF16)
    ffn_w_out_b = ffn_w_out.astype(BF16)

    def ffn_pair(xp, xs, layer, idx, ln_idx):
        w_in = ffn_w_in_b[layer, idx]
        w_out = ffn_w_out_b[layer, idx]
        g, b = row(ln_g[layer, ln_idx]), row(ln_b[layer, ln_idx])
        xp = _ffn(xp.reshape(bp * tp, D_MODEL), w_in, w_out, g, b).reshape(bp, tp, D_MODEL)
        xs = _ffn(xs.reshape(bs * ts, D_MODEL), w_in, w_out, g, b).reshape(bs, ts, D_MODEL)
        return xp, xs

    for layer in range(DEPTH):
        xp, xs = ffn_pair(xp, xs, layer, 0, 0)
        j = layer // 2
        g1, b1 = row(ln_g[layer, 1]), row(ln_b[layer, 1])
        if layer % 2 == 0:
            w_in, bias, alog = _prep_even(even_w_in[j], fox_b_f[j], gdn_a_log[j], gdn_dt_bias[j])
            common = (w_in, bias, alog, gdn_conv_w[j].astype(F32), row(gdn_norm_g[j]), even_w_out[j].astype(BF16), g1, b1)
            s0 = jnp.zeros((bp, GDN_HEADS, GDN_DK, GDN_DV), F32)
            buf0 = jnp.zeros((bp, CONV_W - 1, GDN_CONV_DIM), F32)
            xp, st_p = _even_layer(xp, None, s0, buf0, *common)
            lc = jnp.pad(cache_fox_logf[j].astype(F32), ((0, 0), (0, 0), (0, LANES - FOX_HEADS)))
            fox_past = (cache_fox_k[j].reshape(bs, past, FOX_W), cache_fox_v[j].reshape(bs, past, FOX_W), lc)
            xs, st_s = _even_layer(xs, fox_past, state_gdn[j].astype(F32), state_gdn_conv[j].astype(F32), *common)
            even_p.append(st_p)
            even_s.append(st_s)
        else:
            bre, bim, cre, cim, sw = _prep_odd(s5_lam_re[j], s5_lam_im[j], s5_log_step[j], s5_b_re[j], s5_b_im[j],
                                               s5_c_re[j], s5_c_im[j])
            common = (odd_w_in[j].astype(BF16), bre, bim, cre, cim, sw, row(s5_d[j]), s5_glu_w[j].astype(BF16),
                      row(s5_glu_b[j]), odd_w_out[j].astype(BF16), g1, b1)
            h0 = jnp.zeros((bp, S5_ROWS, LANES), F32)
            xp, hr, hi = _odd(xp, h0, h0, *common)
            odd_p.append((hr.reshape(bp, S5_GROUPS, S5_STATE), hi.reshape(bp, S5_GROUPS, S5_STATE)))
            xs, hr, hi = _odd(xs, state_s5_re[j].astype(F32).reshape(bs, S5_ROWS, LANES),
                              state_s5_im[j].astype(F32).reshape(bs, S5_ROWS, LANES), *common)
            odd_s.append((hr.reshape(bs, S5_GROUPS, S5_STATE), hi.reshape(bs, S5_GROUPS, S5_STATE)))
        xp, xs = ffn_pair(xp, xs, layer, 1, 2)

    stack = lambda states, idx: jnp.stack([s[idx] for s in states])
    return (xp, xs,
            stack(even_p, 0), stack(even_p, 1), stack(even_p, 2), stack(even_p, 3), stack(even_p, 4),
            stack(odd_p, 0), stack(odd_p, 1),
            stack(even_s, 0), stack(even_s, 1), stack(even_s, 2), stack(even_s, 3), stack(even_s, 4),
            stack(odd_s, 0), stack(odd_s, 1))
```

```python
import functools

import jax
import jax.numpy as jnp
from jax import lax
from jax.experimental import pallas as pl
from jax.experimental.pallas import tpu as pltpu

F32 = jnp.float32
BF16 = jnp.bfloat16

D_MODEL = 1024
DEPTH = 2
CHUNK = 64
FOX_HEADS = 8
FOX_DH = 64
FOX_W = FOX_HEADS * FOX_DH
GDN_HEADS = 4
GDN_DK = 128
GDN_DV = 128
GDN_QK = GDN_HEADS * GDN_DK
GDN_VW = GDN_HEADS * GDN_DV
GDN_CONV_DIM = 2 * GDN_QK + GDN_VW
CONV_W = 4
S5_GROUP = 16
S5_GROUPS = D_MODEL // S5_GROUP
S5_STATE = 64
S5_LANES = S5_GROUPS * S5_STATE
D_FF = ((8 * D_MODEL // 3 + 127) // 128) * 128
ALPHA = (2.0 * DEPTH) ** 0.25
LN_EPS = 1e-5
NORM_EPS = 1e-6
EVEN_SPLITS = (FOX_W, FOX_W, FOX_W, FOX_HEADS, GDN_CONV_DIM, GDN_HEADS, GDN_HEADS, GDN_VW)

LANES = 128
SUBLANES = 8
MXU_DIM = 256
VMEM_LIMIT = 56 * 1024 * 1024

FOX_XW = FOX_HEADS * LANES
_Q0 = 0
_K0 = _Q0 + FOX_W
_V0 = _K0 + FOX_W
_X0 = _V0 + FOX_W
_Z0 = _X0 + GDN_CONV_DIM
_G0 = _Z0 + GDN_VW
EVEN_COLS = _G0 + LANES
_BIAS_LANE = FOX_DH
_ONES_ROW = FOX_DH
_LOGF_LANE, _G_LANE, _BETA_LANE, _CUM_LANE = 0, FOX_HEADS, FOX_HEADS + GDN_HEADS, FOX_HEADS + 2 * GDN_HEADS
NEG_BIG = -1e30
LOG2E = 1.4426950408889634

FFN_ROWS = 1024
FFN_COLS = 256
PART_ROWS = 256
EVEN_ROWS = 512
ATT_ROWS = 1024
ATT_HEADS = 2
GDN_ROWS = 512
GDN_GROUP_ROWS = 512
ODD_ROWS = 512
SCAN_UNROLL = 8
S5_ROWS = S5_LANES // LANES
S5_PITCH = S5_ROWS + SUBLANES


def _dot(a, b):
    return jnp.dot(a, b, preferred_element_type=F32)


def _dot_nt(a, b):
    return lax.dot_general(a, b, (((1,), (1,)), ((), ())), preferred_element_type=F32)


def _layer_norm(z, g, b):
    mu = jnp.mean(z, -1, keepdims=True)
    d = z - mu
    var = jnp.mean(d * d, -1, keepdims=True)
    return d * lax.rsqrt(var + LN_EPS) * g + b


def _split3(x):
    hi = x.astype(BF16)
    r = x - hi.astype(F32)
    mid = r.astype(BF16)
    lo = (r - mid.astype(F32)).astype(BF16)
    return hi, mid, lo


def _tri_cumsum(tri, x):
    hi, mid, lo = _split3(x)
    return _dot(tri, hi) + _dot(tri, mid) + _dot(tri, lo)


def _row_parts(rows):
    n = max(1, rows // PART_ROWS)
    return [slice(r * (rows // n), (r + 1) * (rows // n)) for r in range(n)]


def _const_spec(shape):
    nd = len(shape)
    return pl.BlockSpec(shape, lambda *_: (0,) * nd, pipeline_mode=pl.Buffered(1))


def _params(sem):
    return pltpu.CompilerParams(dimension_semantics=sem, vmem_limit_bytes=VMEM_LIMIT)


def _ffn_rows(x_ref, o_ref, h_ref, win_ref, wout_ref, g_ref, b_ref):
    halves = _row_parts(x_ref.shape[0])
    for rows in halves:
        xb = x_ref[rows, :].astype(BF16)
        for c in range(D_FF // FFN_COLS):
            lo = c * FFN_COLS
            a = _dot(xb, win_ref[:, lo:lo + FFN_COLS])
            b = _dot(xb, win_ref[:, D_FF + lo:D_FF + lo + FFN_COLS])
            h_ref[rows, lo:lo + FFN_COLS] = (jax.nn.silu(a) * b).astype(BF16)
    for rows in halves:
        y = _dot(h_ref[rows, :], wout_ref[...])
        o_ref[rows, :] = _layer_norm(ALPHA * x_ref[rows, :] + 0.5 * y, g_ref[...], b_ref[...])


def _ffn_kernel(x_ref, xs_ref, win_ref, wout_ref, g_ref, b_ref, o_ref, os_ref, h_ref):
    _ffn_rows(x_ref, o_ref, h_ref, win_ref, wout_ref, g_ref, b_ref)

    @pl.when(pl.program_id(0) == pl.num_programs(0) - 1)
    def _():
        _ffn_rows(xs_ref, os_ref, h_ref, win_ref, wout_ref, g_ref, b_ref)


def _ffn(x, xs, w_in, w_out, g, b):
    n, ns = x.shape[0], xs.shape[0]
    tm = min(FFN_ROWS, n)
    assert ns <= tm
    return pl.pallas_call(
        _ffn_kernel,
        grid=(n // tm,),
        in_specs=[pl.BlockSpec((tm, D_MODEL), lambda i: (i, 0)),
                  _const_spec((ns, D_MODEL)),
                  _const_spec((D_MODEL, 2 * D_FF)),
                  _const_spec((D_FF, D_MODEL)),
                  _const_spec((1, D_MODEL)),
                  _const_spec((1, D_MODEL))],
        out_specs=[pl.BlockSpec((tm, D_MODEL), lambda i: (i, 0)),
                   pl.BlockSpec((ns, D_MODEL), lambda i: (0, 0))],
        out_shape=[jax.ShapeDtypeStruct((n, D_MODEL), F32), jax.ShapeDtypeStruct((ns, D_MODEL), F32)],
        scratch_shapes=[pltpu.VMEM((tm, D_FF), BF16)],
        compiler_params=_params(("arbitrary",)),
        name="ffn_ln",
    )(x, xs, w_in, w_out, g, b)


def _even_in_kernel(x_ref, w_ref, bias_ref, alog_ref,
                    q_ref, kx_ref, vt_ref, k_ref, v_ref, gx_ref, gz_ref, gates_ref,
                    carry_ref):
    t = pl.program_id(1)
    tm = x_ref.shape[1]

    @pl.when(t == 0)
    def _():
        carry_ref[...] = jnp.zeros_like(carry_ref)

    for rows in _row_parts(tm):
        n = rows.stop - rows.start
        xb = x_ref[0, rows, :].astype(BF16)

        def mm(lo, width):
            return _dot(xb, w_ref[:, lo:lo + width])

        lane = lax.broadcasted_iota(jnp.int32, (n, LANES), 1)
        raw = mm(_G0, LANES) + bias_ref[...]
        logf = jax.nn.log_sigmoid(raw)
        gdec = -jnp.exp(alog_ref[...]) * jax.nn.softplus(raw)
        beta = jax.nn.sigmoid(raw)
        gates = jnp.where(lane < _G_LANE, logf, jnp.where(lane < _BETA_LANE, gdec, jnp.where(lane < _CUM_LANE, beta, logf)))
        row = lax.broadcasted_iota(jnp.int32, (n, n), 0)
        col = lax.broadcasted_iota(jnp.int32, (n, n), 1)
        tri = jnp.where(row >= col, 1.0, 0.0).astype(BF16)
        cum = _tri_cumsum(tri, logf) + carry_ref[...]
        carry_ref[...] = cum[n - 1:n, :]
        gates_ref[0, rows, :] = jnp.where((lane >= _CUM_LANE) & (lane < _CUM_LANE + FOX_HEADS), cum, gates)

        k = mm(_K0, FOX_W)
        k_ref[0, rows, :] = k
        v = mm(_V0, FOX_W)
        v_ref[0, rows, :] = v
        gx_ref[0, rows, :] = mm(_X0, GDN_CONV_DIM)
        gz_ref[0, rows, :] = mm(_Z0, GDN_VW)

        c_parts = [part.astype(F32) for part in _split3(cum * LOG2E)]
        q = mm(_Q0, FOX_W) * (FOX_DH ** -0.5 * LOG2E)
        for h in range(FOX_HEADS):
            pair = slice(LANES * (h // 2), LANES * (h // 2 + 1))
            q_h, k_h = q[:, pair], k[:, pair]
            if h % 2:
                q_h, k_h = pltpu.roll(q_h, FOX_DH, 1), pltpu.roll(k_h, FOX_DH, 1)
            ch = [part[:, _CUM_LANE + h:_CUM_LANE + h + 1] for part in c_parts]
            q_ext = jnp.where(lane < _BIAS_LANE + 3, 0.0, jnp.where(lane < _BIAS_LANE + 6, 1.0, 0.0))
            k_ext = jnp.where(lane < _BIAS_LANE + 3, 1.0, 0.0)
            for i in range(3):
                q_ext = jnp.where(lane == _BIAS_LANE + i, ch[i], q_ext)
                k_ext = jnp.where(lane == _BIAS_LANE + 3 + i, -ch[i], k_ext)
            hs = slice(LANES * h, LANES * (h + 1))
            q_ref[0, rows, hs] = jnp.where(lane < FOX_DH, q_h, q_ext).astype(BF16)
            kx_ref[0, rows, hs] = jnp.where(lane < FOX_DH, k_h, k_ext).astype(BF16)

        v_t = v.T
        pack = 2 * SUBLANES
        ones_blk = jnp.where(lax.broadcasted_iota(jnp.int32, (pack, n), 0) == 0, 1.0, 0.0).astype(BF16)
        for h in range(FOX_HEADS):
            base = LANES * h
            vt_ref[0, base:base + FOX_DH, rows] = v_t[FOX_DH * h:FOX_DH * (h + 1), :].astype(BF16)
            vt_ref[0, base + _ONES_ROW:base + _ONES_ROW + pack, rows] = ones_blk
            vt_ref[0, base + _ONES_ROW + pack:base + LANES, rows] = jnp.zeros((LANES - _ONES_ROW - pack, n), BF16)


def _even_in(x, w, bias, alog):
    bsz, t, _ = x.shape
    tm = min(EVEN_ROWS, t)
    row_spec = lambda width: pl.BlockSpec((1, tm, width), lambda b, i: (b, i, 0))
    shp = lambda width, dt: jax.ShapeDtypeStruct((bsz, t, width), dt)
    return pl.pallas_call(
        _even_in_kernel,
        grid=(bsz, t // tm),
        in_specs=[row_spec(D_MODEL), _const_spec((D_MODEL, EVEN_COLS)), _const_spec((1, LANES)), _const_spec((1, LANES))],
        out_specs=[row_spec(FOX_XW), row_spec(FOX_XW),
                   pl.BlockSpec((1, FOX_XW, tm), lambda b, i: (b, 0, i)),
                   row_spec(FOX_W), row_spec(FOX_W), row_spec(GDN_CONV_DIM), row_spec(GDN_VW), row_spec(LANES)],
        out_shape=[shp(FOX_XW, BF16), shp(FOX_XW, BF16), jax.ShapeDtypeStruct((bsz, FOX_XW, t), BF16),
                   shp(FOX_W, F32), shp(FOX_W, F32), shp(GDN_CONV_DIM, F32), shp(GDN_VW, F32), shp(LANES, F32)],
        scratch_shapes=[pltpu.VMEM((1, LANES), F32)],
        compiler_params=_params(("parallel", "arbitrary")),
        name="even_in",
    )(x, w, bias, alog)


def _fox_prompt_kernel(q_ref, k_ref, vt_ref, o_ref, m_ref, acc_ref, s0_ref, s1_ref, p0_ref, p1_ref, x0_ref, x1_ref):
    qi = pl.program_id(2)
    tq = q_ref.shape[1]
    tk = s0_ref.shape[1]
    nh = q_ref.shape[2] // LANES
    heads = [slice(LANES * a, LANES * (a + 1)) for a in range(nh)]
    m_ref[...] = jnp.full(m_ref.shape, NEG_BIG, F32)
    acc_ref[...] = jnp.zeros_like(acc_ref)
    key_pos = lax.broadcasted_iota(jnp.int32, (tk, tq), 0)
    qry_pos = lax.broadcasted_iota(jnp.int32, (tk, tq), 1)

    upper = slice(tk, tq)
    visible_upper = (lax.broadcasted_iota(jnp.int32, (tk, tq - tk), 0)
                     <= lax.broadcasted_iota(jnp.int32, (tk, tq - tk), 1))
    even = (s0_ref, x0_ref, p0_ref)
    odd = (s1_ref, x1_ref, p1_ref)

    def scores(j, dst, key_shift):
        off = pl.multiple_of(j * tk, tk)
        for a in range(nh):
            if key_shift == tk:
                s_t = _dot_nt(k_ref[0, pl.ds(off, tk), heads[a]], q_ref[0, tk:tq, heads[a]])
                s_t = jnp.where(visible_upper, s_t, NEG_BIG)
                dst[0][a, :, upper] = s_t
                dst[1][a, :, upper] = jnp.broadcast_to(jnp.max(s_t, axis=0, keepdims=True), (SUBLANES, tq - tk))
                continue
            s_t = _dot_nt(k_ref[0, pl.ds(off, tk), heads[a]], q_ref[0, :, heads[a]])
            if key_shift is not None:
                s_t = jnp.where(key_pos + key_shift <= qry_pos, s_t, NEG_BIG)
            dst[0][a] = s_t
            dst[1][a] = jnp.broadcast_to(jnp.max(s_t, axis=0, keepdims=True), (SUBLANES, tq))

    def weighted_values(j, p_src, a):
        off = pl.multiple_of(j * tk, tk)
        return _dot(vt_ref[0, heads[a], pl.ds(off, tk)], p_src[a])

    def step(j, cur, prev, prefetch, next_shift=None):
        if prefetch:
            scores(j + 1, prev, next_shift)
        j_prev = jnp.maximum(j - 1, 0)
        for a in range(nh):
            m_prev = m_ref[a]
            m_new = jnp.maximum(m_prev, cur[1][a])
            r = weighted_values(j_prev, prev[2], a)
            cur[2][a] = jnp.exp2(cur[0][a] - m_new[0:1]).astype(BF16)
            acc_ref[a] = jnp.exp2(m_prev[0:1] - m_new[0:1]) * (acc_ref[a] + r)
            m_ref[a] = m_new

    p1_ref[...] = jnp.zeros_like(p1_ref)

    @pl.when(qi == 0)
    def _():
        scores(0, even, 0)

    @pl.when(qi > 0)
    def _():
        scores(0, even, None)

        def two_tiles(i, c):
            j = 2 * i
            step(j, even, odd, True)
            step(j + 1, odd, even, True)
            return c

        lax.fori_loop(0, qi - 1, two_tiles, 0)
        step(2 * qi - 2, even, odd, True)
        step(2 * qi - 1, odd, even, True, 0)

    step(2 * qi, even, odd, True, tk)

    j_last = 2 * qi + 1
    off_last = pl.multiple_of(j_last * tk, tk)
    for a in range(nh):
        out_rows = slice(FOX_DH * a, FOX_DH * (a + 1))
        r = weighted_values(j_last - 1, p0_ref, a)
        m_prev = m_ref[a, :, upper]
        m_new = jnp.maximum(m_prev, x1_ref[a, :, upper])
        p_t = jnp.exp2(s1_ref[a, :, upper] - m_new[0:1]).astype(BF16)
        lo = acc_ref[a, :, 0:tk] + r[:, 0:tk]
        hi = jnp.exp2(m_prev[0:1] - m_new[0:1]) * (acc_ref[a, :, upper] + r[:, upper])
        hi = hi + _dot(vt_ref[0, heads[a], pl.ds(off_last, tk)], p_t)
        o_ref[0, out_rows, 0:tk] = (lo[0:FOX_DH] / lo[_ONES_ROW:_ONES_ROW + 1]).astype(BF16)
        o_ref[0, out_rows, upper] = (hi[0:FOX_DH] / hi[_ONES_ROW:_ONES_ROW + 1]).astype(BF16)


def _fox_prompt(q, kx, vt):
    bsz, t, _ = q.shape
    tq = min(ATT_ROWS, t)
    nh = ATT_HEADS
    return pl.pallas_call(
        _fox_prompt_kernel,
        grid=(bsz, FOX_HEADS // nh, t // tq),
        in_specs=[pl.BlockSpec((1, tq, nh * LANES), lambda b, p, i: (b, i, p)),
                  pl.BlockSpec((1, t, nh * LANES), lambda b, p, i: (b, 0, p)),
                  pl.BlockSpec((1, nh * LANES, t), lambda b, p, i: (b, p, 0))],
        out_specs=pl.BlockSpec((1, nh * FOX_DH, tq), lambda b, p, i: (b, p, i)),
        out_shape=jax.ShapeDtypeStruct((bsz, FOX_W, t), BF16),
        scratch_shapes=[pltpu.VMEM((nh, SUBLANES, tq), F32), pltpu.VMEM((nh, LANES, tq), F32),
                        pltpu.VMEM((nh, tq // 2, tq), F32), pltpu.VMEM((nh, tq // 2, tq), F32),
                        pltpu.VMEM((nh, tq // 2, tq), BF16), pltpu.VMEM((nh, tq // 2, tq), BF16),
                        pltpu.VMEM((nh, SUBLANES, tq), F32), pltpu.VMEM((nh, SUBLANES, tq), F32)],
        compiler_params=_params(("parallel", "parallel", "arbitrary")),
        name="fox_prompt",
    )(q, kx, vt)


def _fox_sample_kernel(q_ref, kn_ref, vn_ref, gates_ref, kc_ref, vc_ref, lc_ref, o_ref):
    past = kc_ref.shape[1]
    t = q_ref.shape[1]
    row = lax.broadcasted_iota(jnp.int32, (past, past), 0)
    col = lax.broadcasted_iota(jnp.int32, (past, past), 1)
    tri = jnp.where(row >= col, 1.0, 0.0).astype(BF16)
    cum = _tri_cumsum(tri, lc_ref[0])
    ck_cache = (cum - cum[past - 1:past, :]).T
    gts = gates_ref[0]
    ck_new = gts.T
    kc = kc_ref[0].astype(BF16)
    vc = vc_ref[0].astype(BF16)
    kn = kn_ref[0].astype(BF16)
    vn = vn_ref[0].astype(BF16)
    causal = lax.broadcasted_iota(jnp.int32, (t, t), 1) <= lax.broadcasted_iota(jnp.int32, (t, t), 0)
    lane = lax.broadcasted_iota(jnp.int32, (t, LANES), 1)
    outs = []
    for h in range(FOX_HEADS):
        ls = slice(LANES * (h // 2), LANES * (h // 2 + 1))
        qh = q_ref[0, :, LANES * h:LANES * (h + 1)].astype(F32)
        if h % 2 == 0:
            qh = jnp.where(lane < FOX_DH, qh, 0.0).astype(BF16)
        else:
            qh = jnp.where(lane >= FOX_DH, pltpu.roll(qh, FOX_DH, 1), 0.0).astype(BF16)
        cq = gts[:, _CUM_LANE + h:_CUM_LANE + h + 1]
        cq = cq * LOG2E
        s1 = (_dot_nt(qh, kc[:, ls]) + cq) - ck_cache[h:h + 1, :] * LOG2E
        s2 = (_dot_nt(qh, kn[:, ls]) + cq) - ck_new[_CUM_LANE + h:_CUM_LANE + h + 1, :] * LOG2E
        s2 = jnp.where(causal, s2, NEG_BIG)
        m = jnp.maximum(jnp.max(s1, axis=1, keepdims=True), jnp.max(s2, axis=1, keepdims=True))
        p1 = jnp.exp2(s1 - m)
        p2 = jnp.exp2(s2 - m)
        l = jnp.sum(p1, axis=1, keepdims=True) + jnp.sum(p2, axis=1, keepdims=True)
        outs.append((_dot(p1.astype(BF16), vc[:, ls]) + _dot(p2.astype(BF16), vn[:, ls])) / l)
    o = jnp.concatenate([jnp.where(lane < FOX_DH, outs[2 * p], outs[2 * p + 1]) for p in range(FOX_HEADS // 2)], axis=1)
    o_ref[0] = o.T.astype(BF16)


def _fox_sample(q, k, v, gates, past_k, past_v, past_logf):
    bsz, t, _ = k.shape
    past = past_k.shape[1]
    blk = lambda rows, width: pl.BlockSpec((1, rows, width), lambda b: (b, 0, 0))
    return pl.pallas_call(
        _fox_sample_kernel,
        grid=(bsz,),
        in_specs=[blk(t, FOX_XW), blk(t, FOX_W), blk(t, FOX_W), blk(t, LANES),
                  blk(past, FOX_W), blk(past, FOX_W), blk(past, LANES)],
        out_specs=blk(FOX_W, t),
        out_shape=jax.ShapeDtypeStruct((bsz, FOX_W, t), BF16),
        compiler_params=_params(("parallel",)),
        name="fox_sample",
    )(q, k, v, gates, past_k, past_v, past_logf)


def _gdn_kernel(x_ref, gates_ref, z_ref, buf_ref, cw_ref, s0_ref, ng_ref, res_ref, oft_ref, wout_ref, lg_ref, lb_ref,
                y_ref, sfin_ref, cbuf_ref, xbuf, s_scr, o_ref, yf_ref, *, chunk):
    t = pl.program_id(1)
    tc = x_ref.shape[1]
    pad = SUBLANES
    yf_ref[...] = lax.dot_general(oft_ref[0], wout_ref[0:FOX_W, :], (((0,), (0,)), ((), ())),
                                  preferred_element_type=F32)

    @pl.when(t == 0)
    def _():
        xbuf[0:pad, :] = jnp.zeros((pad, GDN_CONV_DIM), F32)
        xbuf[pad - (CONV_W - 1):pad, :] = buf_ref[0]
        s_scr[...] = s0_ref[0]

    @pl.when(t > 0)
    def _():
        xbuf[0:pad, :] = xbuf[tc:tc + pad, :]

    xbuf[pad:pad + tc, :] = x_ref[0]
    cw = cw_ref[...]
    cbuf_ref[0] = xbuf[pad + tc - (CONV_W - 1):pad + tc, :]

    grp = min(tc, GDN_GROUP_ROWS)
    shift = chunk.bit_length() - 1
    row = lax.broadcasted_iota(jnp.int32, (grp, grp), 0)
    col = lax.broadcasted_iota(jnp.int32, (grp, grp), 1)
    tri = jnp.where((row >= col) & ((row >> shift) == (col >> shift)), 1.0, 0.0).astype(BF16)
    r_c = lax.broadcasted_iota(jnp.int32, (chunk, chunk), 0)
    c_c = lax.broadcasted_iota(jnp.int32, (chunk, chunk), 1)
    lower = r_c >= c_c
    strict = r_c > c_c
    eye = jnp.where(r_c == c_c, 1.0, 0.0)
    ng = ng_ref[...]

    heads = range(GDN_HEADS)
    chunks = range(grp // chunk)
    rows = [slice(c * chunk, (c + 1) * chunk) for c in chunks]
    cols = [slice(GDN_DK * h, GDN_DK * (h + 1)) for h in heads]

    def prepare(base):
        conv = cw[0:1] * xbuf[pad - 3 + base:pad - 3 + base + grp, :]
        for j in range(1, CONV_W):
            conv = conv + cw[j:j + 1] * xbuf[pad - 3 + j + base:pad - 3 + j + base + grp, :]
        act = jax.nn.silu(conv)
        gts = gates_ref[0, base:base + grp, :]
        gcum = _tri_cumsum(tri, gts)
        gcum_t = gcum.T

        kn, kn_b, kbeta_b, qn_b, qg, rhs, gc = [], [], [], [], [], [], []
        for h in heads:
            q = act[:, cols[h]]
            k = act[:, GDN_QK + GDN_DK * h:GDN_QK + GDN_DK * (h + 1)]
            v = act[:, 2 * GDN_QK + GDN_DV * h:2 * GDN_QK + GDN_DV * (h + 1)]
            qn = q * lax.rsqrt(jnp.sum(q * q, -1, keepdims=True) + NORM_EPS) * (GDN_DK ** -0.5)
            kn.append(k * lax.rsqrt(jnp.sum(k * k, -1, keepdims=True) + NORM_EPS))
            beta = gts[:, _BETA_LANE + h:_BETA_LANE + h + 1]
            gc.append(gcum[:, _G_LANE + h:_G_LANE + h + 1])
            eg = jnp.exp(gc[h])
            kbeta = kn[h] * beta
            rhs.append(jnp.concatenate([v * beta, kbeta * eg], axis=1).astype(BF16))
            qg.append((qn * eg).astype(BF16))
            kn_b.append(kn[h].astype(BF16))
            kbeta_b.append(kbeta.astype(BF16))
            qn_b.append(qn.astype(BF16))

        units = [(c, h) for c in chunks for h in heads]
        decay, attn, inv, power = {}, {}, {}, {}
        for c, h in units:
            gcc = gc[h][rows[c]]
            gcr = gcum_t[_G_LANE + h:_G_LANE + h + 1, rows[c]]
            decay[c, h] = jnp.where(lower, jnp.exp(jnp.where(lower, gcc - gcr, 0.0)), 0.0)
        for c, h in units:
            power[c, h] = jnp.where(strict, _dot_nt(kbeta_b[h][rows[c]], kn_b[h][rows[c]]) * decay[c, h], 0.0)
            inv[c, h] = eye - power[c, h]
        p = 1
        while 2 * p < chunk:
            for u in units:
                pb = power[u].astype(BF16)
                power[u] = _dot(pb, pb)
            for u in units:
                inv[u] = inv[u] + _dot(inv[u].astype(BF16), power[u].astype(BF16))
            p *= 2
        uw = {}
        for c, h in units:
            uw[c, h] = _dot(inv[c, h].astype(BF16), rhs[h][rows[c]])
        for c, h in units:
            attn[c, h] = (_dot_nt(qn_b[h][rows[c]], kn_b[h][rows[c]]) * decay[c, h]).astype(BF16)
        return base, uw, attn, qg, kn, gc

    def recur(prepared):
        base, uw, attn, qg, kn, gc = prepared
        for c in chunks:
            out_rows = slice(base + c * chunk, base + (c + 1) * chunk)
            state = [s_scr[h] for h in heads]
            g_last = [gc[h][rows[c]][chunk - 1:chunk] for h in heads]
            ws = [_dot(jnp.concatenate([uw[c, h][:, GDN_DV:].astype(BF16), qg[h][rows[c]]], axis=0),
                       state[h].astype(BF16)) for h in heads]
            v_new = [uw[c, h][:, :GDN_DV] - ws[h][:chunk] for h in heads]
            v_new_b = [v_new[h].astype(BF16) for h in heads]
            kdec_t = [(kn[h][rows[c]] * jnp.exp(g_last[h] - gc[h][rows[c]])).T.astype(BF16) for h in heads]
            for h in heads:
                s_scr[h] = state[h] * jnp.exp(g_last[h]) + _dot(kdec_t[h], v_new_b[h])
            for h in heads:
                o = ws[h][chunk:] + _dot(attn[c, h], v_new_b[h])
                z = z_ref[0, out_rows, cols[h]]
                o = o * lax.rsqrt(jnp.mean(o * o, -1, keepdims=True) + NORM_EPS) * ng * jax.nn.silu(z)
                o_ref[0, out_rows, cols[h]] = o.astype(BF16)

    pending = prepare(0)
    for g in range(1, tc // grp):
        recur(pending)
        pending = prepare(g * grp)
    recur(pending)

    for part in _row_parts(tc):
        y = yf_ref[part, :] + _dot(o_ref[0, part, :], wout_ref[FOX_W:FOX_W + GDN_VW, :])
        y_ref[0, part, :] = _layer_norm(ALPHA * res_ref[0, part, :] + y, lg_ref[...], lb_ref[...])

    sfin_ref[0] = s_scr[...]


def _gdn(gx, gates, gz, buf, conv_w, s0, norm_g, x, o_fox_t, w_out, g, b):
    bsz, t, _ = gx.shape
    chunk = min(t, CHUNK)
    tc = min(GDN_ROWS, t)
    row_spec = lambda width: pl.BlockSpec((1, tc, width), lambda b, i: (b, i, 0))
    return pl.pallas_call(
        functools.partial(_gdn_kernel, chunk=chunk),
        grid=(bsz, t // tc),
        in_specs=[row_spec(GDN_CONV_DIM), row_spec(LANES), row_spec(GDN_VW),
                  pl.BlockSpec((1, CONV_W - 1, GDN_CONV_DIM), lambda b, i: (b, 0, 0)),
                  _const_spec((CONV_W, GDN_CONV_DIM)),
                  pl.BlockSpec((1, GDN_HEADS, GDN_DK, GDN_DV), lambda b, i: (b, 0, 0, 0)),
                  _const_spec((1, GDN_DV)),
                  row_spec(D_MODEL),
                  pl.BlockSpec((1, FOX_W, tc), lambda b, i: (b, 0, i)),
                  _const_spec((FOX_W + GDN_VW, D_MODEL)), _const_spec((1, D_MODEL)), _const_spec((1, D_MODEL))],
        out_specs=[row_spec(D_MODEL),
                   pl.BlockSpec((1, GDN_HEADS, GDN_DK, GDN_DV), lambda b, i: (b, 0, 0, 0)),
                   pl.BlockSpec((1, CONV_W - 1, GDN_CONV_DIM), lambda b, i: (b, 0, 0))],
        out_shape=[jax.ShapeDtypeStruct((bsz, t, D_MODEL), F32),
                   jax.ShapeDtypeStruct((bsz, GDN_HEADS, GDN_DK, GDN_DV), F32),
                   jax.ShapeDtypeStruct((bsz, CONV_W - 1, GDN_CONV_DIM), F32)],
        scratch_shapes=[pltpu.VMEM((tc + SUBLANES, GDN_CONV_DIM), F32),
                        pltpu.VMEM((GDN_HEADS, GDN_DK, GDN_DV), F32),
                        pltpu.VMEM((1, tc, GDN_VW), BF16),
                        pltpu.VMEM((tc, D_MODEL), F32)],
        compiler_params=_params(("parallel", "arbitrary")),
        name="gdn_out",
    )(gx, gates, gz, buf, conv_w, s0, norm_g, x, o_fox_t, w_out, g, b)


def _odd_kernel(x_ref, h0r_ref, h0i_ref, win_ref, bre_ref, bim_ref, cre_ref, cim_ref, lam_ref, d_ref,
                gw_ref, gb_ref, wout_ref, g_ref, b_ref,
                o_ref, hr_out, hi_out, tre, tim, hr_scr, hi_scr):
    t = pl.program_id(1)
    tm = x_ref.shape[1]
    nblk = D_MODEL // MXU_DIM
    per = S5_LANES // nblk

    @pl.when(t == 0)
    def _():
        hr_scr[...] = h0r_ref[0]
        hi_scr[...] = h0i_ref[0]

    x = x_ref[0]
    u = _dot(x.astype(BF16), win_ref[...])
    ub = u.astype(BF16)
    jper = per // LANES

    def column(j):
        return pl.ds(j, tm, stride=S5_PITCH)

    for kk in range(nblk):
        uk = ub[:, MXU_DIM * kk:MXU_DIM * (kk + 1)]
        b_re = _dot(uk, bre_ref[kk])
        b_im = _dot(uk, bim_ref[kk])
        for jj in range(jper):
            tre[column(jper * kk + jj), :] = b_re[:, LANES * jj:LANES * (jj + 1)]
            tim[column(jper * kk + jj), :] = b_im[:, LANES * jj:LANES * (jj + 1)]

    lr = lam_ref[0]
    li = lam_ref[1]

    def step(i, carry):
        hr, hi = carry
        rows = pl.ds(pl.multiple_of(i * S5_PITCH, SUBLANES), S5_ROWS)
        nr = (lr * hr - li * hi) + tre[rows, :]
        ni = (lr * hi + li * hr) + tim[rows, :]
        tre[rows, :] = nr
        tim[rows, :] = ni
        return nr, ni

    hr, hi = lax.fori_loop(0, tm, step, (hr_scr[...], hi_scr[...]), unroll=SCAN_UNROLL)
    hr_scr[...] = hr
    hi_scr[...] = hi
    hr_out[0] = hr
    hi_out[0] = hi

    ys = []
    for kk in range(nblk):
        s_re = jnp.concatenate([tre[column(jper * kk + jj), :] for jj in range(jper)], axis=1).astype(BF16)
        s_im = jnp.concatenate([tim[column(jper * kk + jj), :] for jj in range(jper)], axis=1).astype(BF16)
        ys.append(_dot(s_re, cre_ref[kk]) - _dot(s_im, cim_ref[kk]))
    y = jnp.concatenate(ys, axis=1) + d_ref[...] * u
    zz = jax.nn.gelu(y)
    gated = zz * jax.nn.sigmoid(_dot(zz.astype(BF16), gw_ref[...]) + gb_ref[...])
    out = _dot(gated.astype(BF16), wout_ref[...])
    o_ref[0] = _layer_norm(ALPHA * x + out, g_ref[...], b_ref[...])


def _odd(x, h0r, h0i, w_in, bre, bim, cre, cim, sw, d, gw, gb, w_out, g, b):
    bsz, t, _ = x.shape
    tm = min(ODD_ROWS, t)
    nblk = D_MODEL // MXU_DIM
    per = S5_LANES // nblk
    st_spec = pl.BlockSpec((1, S5_ROWS, LANES), lambda bb, i: (bb, 0, 0))
    return pl.pallas_call(
        _odd_kernel,
        grid=(bsz, t // tm),
        in_specs=[pl.BlockSpec((1, tm, D_MODEL), lambda bb, i: (bb, i, 0)), st_spec, st_spec,
                  _const_spec((D_MODEL, D_MODEL)),
                  _const_spec((nblk, MXU_DIM, per)), _const_spec((nblk, MXU_DIM, per)),
                  _const_spec((nblk, per, MXU_DIM)), _const_spec((nblk, per, MXU_DIM)),
                  _const_spec((2, S5_ROWS, LANES)), _const_spec((1, D_MODEL)),
                  _const_spec((D_MODEL, D_MODEL)), _const_spec((1, D_MODEL)), _const_spec((D_MODEL, D_MODEL)),
                  _const_spec((1, D_MODEL)), _const_spec((1, D_MODEL))],
        out_specs=[pl.BlockSpec((1, tm, D_MODEL), lambda bb, i: (bb, i, 0)), st_spec, st_spec],
        out_shape=[jax.ShapeDtypeStruct((bsz, t, D_MODEL), F32),
                   jax.ShapeDtypeStruct((bsz, S5_ROWS, LANES), F32),
                   jax.ShapeDtypeStruct((bsz, S5_ROWS, LANES), F32)],
        scratch_shapes=[pltpu.VMEM((tm * S5_PITCH, LANES), F32), pltpu.VMEM((tm * S5_PITCH, LANES), F32),
                        pltpu.VMEM((S5_ROWS, LANES), F32), pltpu.VMEM((S5_ROWS, LANES), F32)],
        compiler_params=_params(("parallel", "arbitrary")),
        name="odd_s5",
    )(x, h0r, h0i, w_in, bre, bim, cre, cim, sw, d, gw, gb, w_out, g, b)


def _prep_even(w_in, b_f, a_log, dt_bias):
    offs = [0]
    for s in EVEN_SPLITS:
        offs.append(offs[-1] + s)
    fq, fk, fv, ff, gx, ga, gb, gz = [w_in[:, offs[i]:offs[i + 1]] for i in range(len(EVEN_SPLITS))]

    pad = jnp.zeros((D_MODEL, LANES - (2 * FOX_HEADS + 2 * GDN_HEADS)), w_in.dtype)
    w = jnp.concatenate([fq, fk, fv, gx, gz, ff, ga, gb, ff, pad], axis=1).astype(BF16)
    zpad = jnp.zeros((LANES - (2 * FOX_HEADS + 2 * GDN_HEADS),), F32)
    bias = jnp.concatenate([b_f.astype(F32), dt_bias.astype(F32), jnp.zeros((GDN_HEADS,), F32), b_f.astype(F32), zpad])
    alog = jnp.concatenate([jnp.zeros((FOX_HEADS,), F32), a_log.astype(F32),
                            jnp.zeros((LANES - FOX_HEADS - GDN_HEADS,), F32)])
    return w, bias.reshape(1, LANES), alog.reshape(1, LANES)


def _block_diag(blocks):
    nblk, g, r, c = blocks.shape
    eye = jnp.eye(g, dtype=blocks.dtype)
    return jnp.einsum('kgrc,gh->kgrhc', blocks, eye).reshape(nblk, g * r, g * c)


def _prep_odd(lam_re, lam_im, log_step, b_re, b_im, c_re, c_im):
    lam = lax.complex(lam_re.astype(F32), lam_im.astype(F32))
    lam_bar = jnp.exp(lam * jnp.exp(log_step.astype(F32))[:, None])
    b_bar = ((lam_bar - 1.0) / lam)[..., None] * lax.complex(b_re.astype(F32), b_im.astype(F32))
    nblk = D_MODEL // MXU_DIM
    gper = S5_GROUPS // nblk
    bt = jnp.swapaxes(b_bar, 1, 2).reshape(nblk, gper, S5_GROUP, S5_STATE)
    bre = _block_diag(jnp.real(bt)).astype(BF16)
    bim = _block_diag(jnp.imag(bt)).astype(BF16)
    ct_re = jnp.swapaxes(c_re.astype(F32), 1, 2).reshape(nblk, gper, S5_STATE, S5_GROUP)
    ct_im = jnp.swapaxes(c_im.astype(F32), 1, 2).reshape(nblk, gper, S5_STATE, S5_GROUP)
    cre = _block_diag(ct_re).astype(BF16)
    cim = _block_diag(ct_im).astype(BF16)
    lam_flat = lam_bar.reshape(S5_ROWS, LANES)
    lam_ri = jnp.stack([jnp.real(lam_flat), jnp.imag(lam_flat)]).astype(F32)
    return bre, bim, cre, cim, lam_ri


def _even_layer(x, fox_past, s0, buf0, w_in, bias, alog, conv_w, norm_g, w_out, g, b):
    bsz, t, _ = x.shape
    q, kx, vt, k, v, gx, gz, gates = _even_in(x, w_in, bias, alog)
    if fox_past is None:
        o_fox_t = _fox_prompt(q, kx, vt)
    else:
        o_fox_t = _fox_sample(q, k, v, gates, *fox_past)
    y, s_new, new_buf = _gdn(gx, gates, gz, buf0, conv_w, s0, norm_g, x, o_fox_t, w_out, g, b)
    state = (k.reshape(bsz, t, FOX_HEADS, FOX_DH), v.reshape(bsz, t, FOX_HEADS, FOX_DH),
             gates[:, :, :FOX_HEADS], s_new, new_buf)
    return y, state


def kernel(x_prompt, x_sample, cache_fox_k, cache_fox_v, cache_fox_logf, state_gdn, state_gdn_conv, state_s5_re, state_s5_im, ffn_w_in, ffn_w_out, ln_g, ln_b, even_w_in, fox_b_f, gdn_conv_w, gdn_a_log, gdn_dt_bias, gdn_norm_g, even_w_out, odd_w_in, s5_lam_re, s5_lam_im, s5_log_step, s5_b_re, s5_b_im, s5_c_re, s5_c_im, s5_d, s5_glu_w, s5_glu_b, odd_w_out):
    bp, tp, _ = x_prompt.shape
    bs, ts, _ = x_sample.shape
    past = cache_fox_k.shape[2]
    xp, xs = x_prompt, x_sample
    even_p, even_s, odd_p, odd_s = [], [], [], []
    row = lambda a: a.astype(F32).reshape(1, -1)
    ffn_w_in_b = ffn_w_in.astype(BF16)
    ffn_w_out_b = ffn_w_out.astype(BF16)

    def ffn_pair(xp, xs, layer, idx, ln_idx):
        w_in = ffn_w_in_b[layer, idx]
        w_out = ffn_w_out_b[layer, idx]
        g, b = row(ln_g[layer, ln_idx]), row(ln_b[layer, ln_idx])
        yp, ys = _ffn(xp.reshape(bp * tp, D_MODEL), xs.reshape(bs * ts, D_MODEL), w_in, w_out, g, b)
        return yp.reshape(bp, tp, D_MODEL), ys.reshape(bs, ts, D_MODEL)

    for layer in range(DEPTH):
        xp, xs = ffn_pair(xp, xs, layer, 0, 0)
        j = layer // 2
        g1, b1 = row(ln_g[layer, 1]), row(ln_b[layer, 1])
        if layer % 2 == 0:
            w_in, bias, alog = _prep_even(even_w_in[j], fox_b_f[j], gdn_a_log[j], gdn_dt_bias[j])
            common = (w_in, bias, alog, gdn_conv_w[j].astype(F32), row(gdn_norm_g[j]), even_w_out[j].astype(BF16), g1, b1)
            s0 = jnp.zeros((bp, GDN_HEADS, GDN_DK, GDN_DV), F32)
            buf0 = jnp.zeros((bp, CONV_W - 1, GDN_CONV_DIM), F32)
            xp, st_p = _even_layer(xp, None, s0, buf0, *common)
            lc = jnp.pad(cache_fox_logf[j].astype(F32), ((0, 0), (0, 0), (0, LANES - FOX_HEADS)))
            fox_past = (cache_fox_k[j].reshape(bs, past, FOX_W), cache_fox_v[j].reshape(bs, past, FOX_W), lc)
            xs, st_s = _even_layer(xs, fox_past, state_gdn[j].astype(F32), state_gdn_conv[j].astype(F32), *common)
            even_p.append(st_p)
            even_s.append(st_s)
        else:
            bre, bim, cre, cim, sw = _prep_odd(s5_lam_re[j], s5_lam_im[j], s5_log_step[j], s5_b_re[j], s5_b_im[j],
                                               s5_c_re[j], s5_c_im[j])
            common = (odd_w_in[j].astype(BF16), bre, bim, cre, cim, sw, row(s5_d[j]), s5_glu_w[j].astype(BF16),
                      row(s5_glu_b[j]), odd_w_out[j].astype(BF16), g1, b1)
            h0 = jnp.zeros((bp, S5_ROWS, LANES), F32)
            xp, hr, hi = _odd(xp, h0, h0, *common)
            odd_p.append((hr.reshape(bp, S5_GROUPS, S5_STATE), hi.reshape(bp, S5_GROUPS, S5_STATE)))
            xs, hr, hi = _odd(xs, state_s5_re[j].astype(F32).reshape(bs, S5_ROWS, LANES),
                              state_s5_im[j].astype(F32).reshape(bs, S5_ROWS, LANES), *common)
            odd_s.append((hr.reshape(bs, S5_GROUPS, S5_STATE), hi.reshape(bs, S5_GROUPS, S5_STATE)))
        xp, xs = ffn_pair(xp, xs, layer, 1, 2)

    stack = lambda states, idx: jnp.stack([s[idx] for s in states])
    return (xp, xs,
            stack(even_p, 0), stack(even_p, 1), stack(even_p, 2), stack(even_p, 3), stack(even_p, 4),
            stack(odd_p, 0), stack(odd_p, 1),
            stack(even_s, 0), stack(even_s, 1), stack(even_s, 2), stack(even_s, 3), stack(even_s, 4),
            stack(odd_s, 0), stack(odd_s, 1))
```
